```python
import math
import jax, jax.numpy as jnp
from jax import lax
import numpy as np

D_MODEL = 2048
BATCH = 4
SEQ = 4096
DEPTH = 1

CHUNK = 64
LEFT_CHUNKS = 8
BAND = (LEFT_CHUNKS + 1) * CHUNK
MEM_LEN = 256
A_HEADS = 16
A_HEAD_DIM = 64
A_WIDTH = A_HEADS * A_HEAD_DIM
MAX_REL = 128
B_HEADS = 16
QK_NOPE = 128
QK_ROPE = 64
V_HEAD = 128
Q_LORA = 512
KV_LORA = 512
ROPE_THETA = 10000.0
B_WIDTH = B_HEADS * V_HEAD
Q_BLOCK = 128
X_HEADS = 4
X_HEAD_DIM = D_MODEL // X_HEADS
D_FF = 5504
N_BRANCH = 2
IN_SPLITS = (A_WIDTH, 2 * A_WIDTH, 3 * A_WIDTH, 3 * A_WIDTH + Q_LORA,
             3 * A_WIDTH + Q_LORA + KV_LORA + QK_ROPE)
IN_COLS = 3 * A_WIDTH + Q_LORA + KV_LORA + QK_ROPE + N_BRANCH * D_MODEL
ALPHA = (2.0 * DEPTH) ** 0.25
BETA = (8.0 * DEPTH) ** -0.25
LN_EPS = 1e-5
RMS_EPS = 1e-6
NEG_INF = -1e30

kernel_name = 'hybrid_chunked_mla_macaron_deepnorm'


def layer_norm(x, g, b):
    xf = x.astype(jnp.float32)
    mu = xf.mean(-1, keepdims=True)
    var = jnp.square(xf - mu).mean(-1, keepdims=True)
    y = (xf - mu) * lax.rsqrt(var + LN_EPS) * g.astype(jnp.float32) + b.astype(jnp.float32)
    return y.astype(x.dtype)


def rms_norm(x, g):
    xf = x.astype(jnp.float32)
    y = xf * lax.rsqrt(jnp.square(xf).mean(-1, keepdims=True) + RMS_EPS) * g.astype(jnp.float32)
    return y.astype(x.dtype)


def swiglu_ffn(x, w_in, w_out):
    g, u = jnp.split(x @ w_in, 2, axis=-1)
    return (jax.nn.silu(g) * u) @ w_out


def rope_tables(seq_len, dim):
    inv = 1.0 / (ROPE_THETA ** (jnp.arange(0, dim, 2, dtype=jnp.float32) / dim))
    ang = jnp.arange(seq_len, dtype=jnp.float32)[:, None] * inv[None, :]
    return jnp.cos(ang)[:, None, :], jnp.sin(ang)[:, None, :]


def apply_rope(x, cos, sin):
    x1, x2 = jnp.split(x.astype(jnp.float32), 2, axis=-1)
    return jnp.concatenate([x1 * cos - x2 * sin, x2 * cos + x1 * sin], axis=-1).astype(x.dtype)


def chunked_relbias_attention(q, k, v, rel_bias):
    b, s, h, d = q.shape
    nc = s // CHUNK
    qc = q.reshape(b, nc, CHUNK, h, d)

    def band(t):
        tc = t.reshape(b, nc, CHUNK, h, d)
        tp = jnp.pad(tc, ((0, 0), (LEFT_CHUNKS, 0), (0, 0), (0, 0), (0, 0)))
        return jnp.concatenate([tp[:, j:j + nc] for j in range(LEFT_CHUNKS + 1)], axis=2)

    kb, vb = band(k), band(v)
    scores = jnp.einsum('bcqhd,bckhd->bchqk', qc, kb).astype(jnp.float32) / math.sqrt(d)
    qi = jnp.arange(CHUNK)[:, None]
    kj = jnp.arange(BAND)[None, :]
    dist = LEFT_CHUNKS * CHUNK + qi - kj
    idx = jnp.clip(dist, -MAX_REL, MAX_REL) + MAX_REL
    bias = rel_bias.astype(jnp.float32)[:, idx]
    key_chunk = jnp.arange(nc)[:, None] - LEFT_CHUNKS + kj // CHUNK
    valid = key_chunk >= 0
    scores = jnp.where(valid[None, :, None, None, :], scores + bias[None, None], NEG_INF)
    p = jax.nn.softmax(scores, axis=-1).astype(v.dtype)
    o = jnp.einsum('bchqk,bckhd->bcqhd', p, vb)
    return o.reshape(b, s, h * d)


def mla_attention(q_lat, kv_lat, q_a_norm, w_q_b, kv_a_norm, w_kv_b, cos, sin):
    b, s, _ = q_lat.shape
    c_q = rms_norm(q_lat, q_a_norm)
    q = (c_q @ w_q_b).reshape(b, s, B_HEADS, QK_NOPE + QK_ROPE)
    q_nope, q_pe = jnp.split(q, [QK_NOPE], axis=-1)
    q = jnp.concatenate([q_nope, apply_rope(q_pe, cos, sin)], axis=-1)
    c_kv, k_pe = jnp.split(kv_lat, [KV_LORA], axis=-1)
    c_kv = rms_norm(c_kv, kv_a_norm)
    kv = (c_kv @ w_kv_b).reshape(b, s, B_HEADS, QK_NOPE + V_HEAD)
    k_nope, v = jnp.split(kv, [QK_NOPE], axis=-1)
    k_pe = apply_rope(k_pe[:, :, None, :], cos, sin)
    k = jnp.concatenate([k_nope, jnp.broadcast_to(k_pe, (b, s, B_HEADS, QK_ROPE))], axis=-1)
    scale = (QK_NOPE + QK_ROPE) ** -0.5
    nb = s // Q_BLOCK
    q_blocks = jnp.moveaxis(q.reshape(b, nb, Q_BLOCK, B_HEADS, QK_NOPE + QK_ROPE), 1, 0)
    key_chunk = jnp.arange(s) // CHUNK

    def block(args):
        qb, start = args
        sc = jnp.einsum('bqhd,bkhd->bhqk', qb, k).astype(jnp.float32) * scale
        q_chunk = (start + jnp.arange(Q_BLOCK)) // CHUNK
        allowed = key_chunk[None, :] <= q_chunk[:, None]
        sc = jnp.where(allowed[None, None], sc, NEG_INF)
        p = jax.nn.softmax(sc, axis=-1).astype(v.dtype)
        return jnp.einsum('bhqk,bkhd->bqhd', p, v)

    starts = jnp.arange(nb, dtype=jnp.int32) * Q_BLOCK
    o = lax.map(block, (q_blocks, starts))
    return jnp.moveaxis(o, 0, 1).reshape(b, s, B_WIDTH)


def hybrid_mixer(u, w_in, gate_bias, rel_bias, q_a_norm, w_q_b, kv_a_norm, w_kv_b,
                 w_o_a, w_o_b, w_out, cos, sin):
    b, s, _ = u.shape
    proj = u @ w_in
    qa, ka, va, q_lat, kv_lat, gates = jnp.split(proj, list(IN_SPLITS), axis=-1)
    shp = (b, s, A_HEADS, A_HEAD_DIM)
    y_a = chunked_relbias_attention(qa.reshape(shp), ka.reshape(shp), va.reshape(shp), rel_bias) @ w_o_a
    y_b = mla_attention(q_lat, kv_lat, q_a_norm, w_q_b, kv_a_norm, w_kv_b, cos, sin) @ w_o_b
    g = jax.nn.sigmoid((gates + gate_bias).astype(jnp.float32)).astype(u.dtype)
    g_a, g_b = jnp.split(g, 2, axis=-1)
    return (g_a * y_a + g_b * y_b) @ w_out


def memory_cross_attention(h, mem, mem_ln_g, mem_ln_b, w_xq, w_xkv, w_xo):
    b, s, _ = h.shape
    m_len = mem.shape[1]
    m = layer_norm(mem, mem_ln_g, mem_ln_b)
    q = (h @ w_xq).reshape(b, s, X_HEADS, X_HEAD_DIM)
    kv = (m @ w_xkv).reshape(b, m_len, 2, X_HEADS, X_HEAD_DIM)
    k, v = kv[:, :, 0], kv[:, :, 1]
    sc = jnp.einsum('bqhd,bkhd->bhqk', q, k).astype(jnp.float32) / math.sqrt(X_HEAD_DIM)
    p = jax.nn.softmax(sc, axis=-1).astype(v.dtype)
    o = jnp.einsum('bhqk,bkhd->bqhd', p, v).reshape(b, s, D_MODEL)
    return o @ w_xo


def setup_inputs(seed: int = 0) -> dict:
    key = jax.random.key(seed)
    ks = iter(jax.random.split(key, 40))

    def nrm(shape, scale):
        return jax.random.normal(next(ks), shape, jnp.float32) * scale

    def gain(shape):
        return 1.0 + nrm(shape, 0.02)

    L, D = DEPTH, D_MODEL
    x = nrm((BATCH, SEQ, D), 1.0)
    mem = nrm((BATCH, MEM_LEN, D), 1.0)
    ffn1_w_in = nrm((L, D, 2 * D_FF), D ** -0.5)
    ffn1_w_out = nrm((L, D_FF, D), BETA * D_FF ** -0.5)
    ln_ffn1_g = gain((L, D))
    ln_ffn1_b = nrm((L, D), 0.02)
    w_in = jnp.concatenate([
        nrm((L, D, 2 * A_WIDTH), D ** -0.5),
        nrm((L, D, A_WIDTH), BETA * D ** -0.5),
        nrm((L, D, Q_LORA + KV_LORA + QK_ROPE), D ** -0.5),
        nrm((L, D, N_BRANCH * D), D ** -0.5),
    ], axis=-1)
    gate_bias = nrm((L, N_BRANCH * D), 0.02)
    rel_bias = nrm((L, A_HEADS, 2 * MAX_REL + 1), 0.3)
    q_a_norm = gain((L, Q_LORA))
    w_q_b = nrm((L, Q_LORA, B_HEADS * (QK_NOPE + QK_ROPE)), Q_LORA ** -0.5)
    kv_a_norm = gain((L, KV_LORA))
    kv_scale = jnp.concatenate([jnp.ones((QK_NOPE,), jnp.float32), jnp.full((V_HEAD,), BETA, jnp.float32)])
    w_kv_b = (nrm((L, KV_LORA, B_HEADS, QK_NOPE + V_HEAD), KV_LORA ** -0.5) * kv_scale
              ).reshape(L, KV_LORA, B_HEADS * (QK_NOPE + V_HEAD))
    w_o_a = nrm((L, A_WIDTH, D), BETA * A_WIDTH ** -0.5)
    w_o_b = nrm((L, B_WIDTH, D), BETA * B_WIDTH ** -0.5)
    w_out = nrm((L, D, D), BETA * D ** -0.5)
    ln_mix_g = gain((L, D))
    ln_mix_b = nrm((L, D), 0.02)
    mem_ln_g = gain((L, D))
    mem_ln_b = nrm((L, D), 0.02)
    w_xq = nrm((L, D, D), D ** -0.5)
    xkv_scale = jnp.array([1.0, BETA], jnp.float32)[:, None, None]
    w_xkv = (nrm((L, D, 2, X_HEADS, X_HEAD_DIM), D ** -0.5) * xkv_scale).reshape(L, D, 2 * D)
    w_xo = nrm((L, D, D), BETA * D ** -0.5)
    ln_x_g = gain((L, D))
    ln_x_b = nrm((L, D), 0.02)
    ffn2_w_in = nrm((L, D, 2 * D_FF), D ** -0.5)
    ffn2_w_out = nrm((L, D_FF, D), BETA * D_FF ** -0.5)
    ln_ffn2_g = gain((L, D))
    ln_ffn2_b = nrm((L, D), 0.02)
    return {'x': x, 'mem': mem,
            'ffn1_w_in': ffn1_w_in, 'ffn1_w_out': ffn1_w_out, 'ln_ffn1_g': ln_ffn1_g, 'ln_ffn1_b': ln_ffn1_b,
            'w_in': w_in, 'gate_bias': gate_bias, 'rel_bias': rel_bias,
            'q_a_norm': q_a_norm, 'w_q_b': w_q_b, 'kv_a_norm': kv_a_norm, 'w_kv_b': w_kv_b,
            'w_o_a': w_o_a, 'w_o_b': w_o_b, 'w_out': w_out, 'ln_mix_g': ln_mix_g, 'ln_mix_b': ln_mix_b,
            'mem_ln_g': mem_ln_g, 'mem_ln_b': mem_ln_b, 'w_xq': w_xq, 'w_xkv': w_xkv, 'w_xo': w_xo,
            'ln_x_g': ln_x_g, 'ln_x_b': ln_x_b,
            'ffn2_w_in': ffn2_w_in, 'ffn2_w_out': ffn2_w_out, 'ln_ffn2_g': ln_ffn2_g, 'ln_ffn2_b': ln_ffn2_b}


def reference(x, mem, ffn1_w_in, ffn1_w_out, ln_ffn1_g, ln_ffn1_b,
              w_in, gate_bias, rel_bias, q_a_norm, w_q_b, kv_a_norm, w_kv_b,
              w_o_a, w_o_b, w_out, ln_mix_g, ln_mix_b,
              mem_ln_g, mem_ln_b, w_xq, w_xkv, w_xo, ln_x_g, ln_x_b,
              ffn2_w_in, ffn2_w_out, ln_ffn2_g, ln_ffn2_b):
    cos, sin = rope_tables(x.shape[1], QK_ROPE)
    h = x
    for l in range(DEPTH):
        h = layer_norm(ALPHA * h + 0.5 * swiglu_ffn(h, ffn1_w_in[l], ffn1_w_out[l]), ln_ffn1_g[l], ln_ffn1_b[l])
        mix = hybrid_mixer(h, w_in[l], gate_bias[l], rel_bias[l], q_a_norm[l], w_q_b[l], kv_a_norm[l],
                           w_kv_b[l], w_o_a[l], w_o_b[l], w_out[l], cos, sin)
        h = layer_norm(ALPHA * h + mix, ln_mix_g[l], ln_mix_b[l])
        xa = memory_cross_attention(h, mem, mem_ln_g[l], mem_ln_b[l], w_xq[l], w_xkv[l], w_xo[l])
        h = layer_norm(ALPHA * h + xa, ln_x_g[l], ln_x_b[l])
        h = layer_norm(ALPHA * h + 0.5 * swiglu_ffn(h, ffn2_w_in[l], ffn2_w_out[l]), ln_ffn2_g[l], ln_ffn2_b[l])
    return h
```

```python
import functools
import math

import jax
import jax.numpy as jnp
from jax import lax
from jax.experimental import pallas as pl
from jax.experimental.pallas import tpu as pltpu

BF16 = jnp.bfloat16
F32 = jnp.float32

D_MODEL = 2048
CHUNK = 64
LEFT_CHUNKS = 8
MEM_LEN = 256
A_HEADS = 16
A_HEAD_DIM = 64
A_WIDTH = A_HEADS * A_HEAD_DIM
MAX_REL = 128
B_HEADS = 16
QK_NOPE = 128
QK_ROPE = 64
V_HEAD = 128
Q_LORA = 512
KV_LORA = 512
ROPE_THETA = 10000.0
B_WIDTH = B_HEADS * V_HEAD
X_HEADS = 4
X_HEAD_DIM = D_MODEL // X_HEADS
D_FF = 5504
DEPTH = 1
ALPHA = (2.0 * DEPTH) ** 0.25
LN_EPS = 1e-5
RMS_EPS = 1e-6
NEG_INF = -1e30

LANE = 128
VMEM_LIMIT_BYTES = 56 * 1024 * 1024

FF_TILE = 512
D_FF_PAD = ((D_FF + FF_TILE - 1) // FF_TILE) * FF_TILE
ROW_TILE = 512
A_Q_TILE = 256
A_LEFT = LEFT_CHUNKS * CHUNK
A_WIN = A_LEFT + A_Q_TILE
B_TILE = 256
LAT_COLS = Q_LORA + KV_LORA + 2 * LANE


def _cparams(semantics):
    return pltpu.CompilerParams(dimension_semantics=semantics, vmem_limit_bytes=VMEM_LIMIT_BYTES)


def _layer_norm(x, g, b):
    mu = jnp.mean(x, axis=-1, keepdims=True)
    xc = x - mu
    var = jnp.mean(xc * xc, axis=-1, keepdims=True)
    return xc * lax.rsqrt(var + LN_EPS) * g + b


def _rms_norm(x, g):
    return x * lax.rsqrt(jnp.mean(x * x, axis=-1, keepdims=True) + RMS_EPS) * g


def _dot(a, b):
    return jnp.dot(a, b, preferred_element_type=F32)


def _dot_nt(a, b):
    return lax.dot_general(a, b, (((1,), (1,)), ((), ())), preferred_element_type=F32)


def _ffn_body(x_ref, wgu_ref, wo_ref, g_ref, b_ref, o_ref, xb_ref, acc_ref):
    j = pl.program_id(1)

    @pl.when(j == 0)
    def _():
        xb_ref[...] = x_ref[...].astype(BF16)
        acc_ref[...] = jnp.zeros_like(acc_ref)

    gu = _dot(xb_ref[...], wgu_ref[...])
    gate = gu[:, :FF_TILE]
    up = gu[:, FF_TILE:]
    act = (gate * jax.nn.sigmoid(gate)) * up
    acc_ref[...] += _dot(act.astype(BF16), wo_ref[...])

    @pl.when(j == pl.num_programs(1) - 1)
    def _():
        y = ALPHA * x_ref[...] + 0.5 * acc_ref[...]
        o_ref[...] = _layer_norm(y, g_ref[...], b_ref[...])


def _ffn_weights(w_in, w_out):
    pad = D_FF_PAD - D_FF
    n_tiles = D_FF_PAD // FF_TILE
    wg = jnp.pad(w_in[:, :D_FF].astype(BF16), ((0, 0), (0, pad))).reshape(D_MODEL, n_tiles, FF_TILE)
    wu = jnp.pad(w_in[:, D_FF:].astype(BF16), ((0, 0), (0, pad))).reshape(D_MODEL, n_tiles, FF_TILE)
    wgu = jnp.concatenate([wg, wu], axis=2).reshape(D_MODEL, 2 * D_FF_PAD)
    wo = jnp.pad(w_out.astype(BF16), ((0, pad), (0, 0)))
    return wgu, wo


def _ffn_ln(x, w_in, w_out, ln_g, ln_b):
    m = x.shape[0]
    tm = min(ROW_TILE, m)
    wgu, wo = _ffn_weights(w_in, w_out)
    return pl.pallas_call(
        _ffn_body,
        grid=(m // tm, D_FF_PAD // FF_TILE),
        in_specs=[
            pl.BlockSpec((tm, D_MODEL), lambda i, j: (i, 0)),
            pl.BlockSpec((D_MODEL, 2 * FF_TILE), lambda i, j: (0, j)),
            pl.BlockSpec((FF_TILE, D_MODEL), lambda i, j: (j, 0)),
            pl.BlockSpec((1, D_MODEL), lambda i, j: (0, 0)),
            pl.BlockSpec((1, D_MODEL), lambda i, j: (0, 0)),
        ],
        out_specs=pl.BlockSpec((tm, D_MODEL), lambda i, j: (i, 0)),
        out_shape=jax.ShapeDtypeStruct((m, D_MODEL), F32),
        scratch_shapes=[pltpu.VMEM((tm, D_MODEL), BF16), pltpu.VMEM((tm, D_MODEL), F32)],
        compiler_params=_cparams(("parallel", "arbitrary")),
        name="ffn_ln",
    )(x, wgu, wo, ln_g.reshape(1, -1), ln_b.reshape(1, -1))


def _mm_body(*refs, pro, has_bias, act, out_scale):
    it = iter(refs)
    x_ref = next(it)
    pg_ref = next(it) if pro else None
    pb_ref = next(it) if pro == "ln" else None
    w_ref = next(it)
    bias_ref = next(it) if has_bias else None
    o_ref = next(it)
    x = x_ref[...]
    if pro == "ln":
        x = _layer_norm(x.astype(F32), pg_ref[...], pb_ref[...])
    elif pro == "rms":
        x = _rms_norm(x.astype(F32), pg_ref[...])
    y = _dot(x.astype(BF16), w_ref[...])
    if has_bias:
        y = y + bias_ref[...]
    if act == "sigmoid":
        y = jax.nn.sigmoid(y)
    if out_scale is not None:
        y = y * out_scale
    o_ref[...] = y.astype(o_ref.dtype)


def _matmul(x, w, *, out_dtype, tn, x_cols=None, pro=None, pro_g=None, pro_b=None,
            bias=None, act=None, out_scale=None, name="matmul"):
    m = x.shape[0]
    k, n = w.shape
    tm = min(ROW_TILE, m)
    xblk = 0 if x_cols is None else x_cols
    args = [x]
    specs = [pl.BlockSpec((tm, k), lambda i, j: (i, xblk))]
    if pro:
        args.append(pro_g.reshape(1, k))
        specs.append(pl.BlockSpec((1, k), lambda i, j: (0, 0)))
    if pro == "ln":
        args.append(pro_b.reshape(1, k))
        specs.append(pl.BlockSpec((1, k), lambda i, j: (0, 0)))
    args.append(w)
    specs.append(pl.BlockSpec((k, tn), lambda i, j: (0, j)))
    if bias is not None:
        args.append(bias.reshape(1, n))
        specs.append(pl.BlockSpec((1, tn), lambda i, j: (0, j)))
    body = functools.partial(_mm_body, pro=pro, has_bias=bias is not None, act=act, out_scale=out_scale)
    return pl.pallas_call(
        body,
        grid=(m // tm, n // tn),
        in_specs=specs,
        out_specs=pl.BlockSpec((tm, tn), lambda i, j: (i, j)),
        out_shape=jax.ShapeDtypeStruct((m, n), out_dtype),
        compiler_params=_cparams(("parallel", "parallel")),
        name=name,
    )(*args)


def _q_up_body(lat_ref, g_ref, w_ref, cos_ref, sin_ref, qn_ref, qp_ref, *, scale):
    c = _rms_norm(lat_ref[...], g_ref[...]).astype(BF16)
    qn_ref[...] = (_dot(c, w_ref[:, :B_WIDTH]) * scale).astype(BF16)
    pa = _dot(c, w_ref[:, B_WIDTH:2 * B_WIDTH])
    pb = _dot(c, w_ref[:, 2 * B_WIDTH:])
    cos = cos_ref[...] * scale
    sin = sin_ref[...] * scale
    for h in range(B_HEADS):
        sl = slice(h * LANE, (h + 1) * LANE)
        qp_ref[:, sl] = (pa[:, sl] * cos + pb[:, sl] * sin).astype(BF16)


def _kv_up_body(c_ref, g_ref, w_ref, ka_ref, kb_ref, cos_ref, sin_ref, kv_ref, kpe_ref):
    c = _rms_norm(c_ref[...], g_ref[...]).astype(BF16)
    n_chunk = 1024
    for n in range(w_ref.shape[1] // n_chunk):
        sl = slice(n * n_chunk, (n + 1) * n_chunk)
        kv_ref[:, sl] = _dot(c, w_ref[:, sl]).astype(BF16)
    kpe_ref[...] = (ka_ref[...] * cos_ref[...] + kb_ref[...] * sin_ref[...]).astype(BF16)


def _rope_tables(seq_len):
    half = QK_ROPE // 2
    inv = 1.0 / (ROPE_THETA ** (jnp.arange(0, QK_ROPE, 2, dtype=F32) / QK_ROPE))
    ang = jnp.arange(seq_len, dtype=F32)[:, None] * inv[None, :]
    zeros = jnp.zeros((seq_len, LANE - 2 * half), F32)
    cos = jnp.concatenate([jnp.cos(ang), jnp.cos(ang), zeros], axis=1)
    sin = jnp.concatenate([jnp.sin(ang), jnp.sin(ang), zeros], axis=1)
    return cos, sin


def _rot_pair(x1, x2, pad_shape):
    zeros = jnp.zeros(pad_shape, x1.dtype)
    return (jnp.concatenate([x1, x2, zeros], axis=-1), jnp.concatenate([-x2, x1, zeros], axis=-1))


def _latent_up(lat, q_a_norm, w_q_b, kv_a_norm, w_kv_b, seq_len):
    m = lat.shape[0]
    tm = min(ROW_TILE, m)
    n_pos = seq_len // tm
    cos, sin = _rope_tables(seq_len)
    half = QK_ROPE // 2
    wq = w_q_b.reshape(Q_LORA, B_HEADS, QK_NOPE + QK_ROPE)
    pe = wq[:, :, QK_NOPE:]
    pe_a, pe_b = _rot_pair(pe[..., :half], pe[..., half:], (Q_LORA, B_HEADS, LANE - QK_ROPE))
    w_q_all = jnp.concatenate([wq[:, :, :QK_NOPE].reshape(Q_LORA, B_WIDTH),
                               pe_a.reshape(Q_LORA, B_WIDTH), pe_b.reshape(Q_LORA, B_WIDTH)],
                              axis=1).astype(BF16)
    wkv = w_kv_b.reshape(KV_LORA, B_HEADS, QK_NOPE + V_HEAD)
    w_kv_all = jnp.concatenate([wkv[:, :, :QK_NOPE].reshape(KV_LORA, B_WIDTH),
                                wkv[:, :, QK_NOPE:].reshape(KV_LORA, B_WIDTH)], axis=1).astype(BF16)
    scale = (QK_NOPE + QK_ROPE) ** -0.5
    row = lambda i: (i, 0)
    const = lambda i: (0, 0)
    pos = lambda i: (i % n_pos, 0)
    q_nope, q_pe = pl.pallas_call(
        functools.partial(_q_up_body, scale=scale),
        grid=(m // tm,),
        in_specs=[
            pl.BlockSpec((tm, Q_LORA), row),
            pl.BlockSpec((1, Q_LORA), const),
            pl.BlockSpec((Q_LORA, 3 * B_WIDTH), const),
            pl.BlockSpec((tm, LANE), pos),
            pl.BlockSpec((tm, LANE), pos),
        ],
        out_specs=[pl.BlockSpec((tm, B_WIDTH), row), pl.BlockSpec((tm, B_WIDTH), row)],
        out_shape=[jax.ShapeDtypeStruct((m, B_WIDTH), BF16), jax.ShapeDtypeStruct((m, B_WIDTH), BF16)],
        compiler_params=_cparams(("parallel",)),
        name="q_up",
    )(lat, q_a_norm.reshape(1, -1), w_q_all, cos, sin)
    ka_blk = (Q_LORA + KV_LORA) // LANE
    kv, k_pe = pl.pallas_call(
        _kv_up_body,
        grid=(m // tm,),
        in_specs=[
            pl.BlockSpec((tm, KV_LORA), lambda i: (i, Q_LORA // KV_LORA)),
            pl.BlockSpec((1, KV_LORA), const),
            pl.BlockSpec((KV_LORA, 2 * B_WIDTH), const),
            pl.BlockSpec((tm, LANE), lambda i: (i, ka_blk)),
            pl.BlockSpec((tm, LANE), lambda i: (i, ka_blk + 1)),
            pl.BlockSpec((tm, LANE), pos),
            pl.BlockSpec((tm, LANE), pos),
        ],
        out_specs=[pl.BlockSpec((tm, 2 * B_WIDTH), row), pl.BlockSpec((tm, LANE), row)],
        out_shape=[jax.ShapeDtypeStruct((m, 2 * B_WIDTH), BF16), jax.ShapeDtypeStruct((m, LANE), BF16)],
        compiler_params=_cparams(("parallel",)),
        name="kv_up",
    )(lat, kv_a_norm.reshape(1, -1), w_kv_all, lat, lat, cos, sin)
    return q_nope, q_pe, kv, k_pe


def _attn_a_body(q_ref, k0_ref, k1_ref, k2_ref, v0_ref, v1_ref, v2_ref, bias_ref, o_ref):
    i = pl.program_id(1)
    col = lax.broadcasted_iota(jnp.int32, (A_Q_TILE, A_WIN), 1)
    in_seq = col >= (A_LEFT // A_Q_TILE - i) * A_Q_TILE
    lane = lax.broadcasted_iota(jnp.int32, (1, LANE), 1)
    low = lane < A_HEAD_DIM
    for hp in range(A_WIDTH // LANE):
        sl = slice(hp * LANE, (hp + 1) * LANE)
        qp = q_ref[0, :, sl]
        kp = jnp.concatenate([k0_ref[0, :, sl], k1_ref[0, :, sl], k2_ref[0, :, sl]], axis=0)
        vp = jnp.concatenate([v0_ref[0, :, sl], v1_ref[0, :, sl], v2_ref[0, :, sl]], axis=0)
        outs = []
        for sub in range(2):
            sel = low if sub == 0 else jnp.logical_not(low)
            qm = jnp.where(sel, qp, jnp.zeros_like(qp))
            s = _dot_nt(qm, kp)
            s = jnp.where(in_seq, s + bias_ref[2 * hp + sub], NEG_INF)
            mx = jnp.max(s, axis=-1, keepdims=True)
            e = jnp.exp(s - mx)
            den = jnp.sum(e, axis=-1, keepdims=True)
            outs.append(_dot(e.astype(BF16), vp) / den)
        o_ref[0, :, sl] = jnp.where(low, outs[0], outs[1]).astype(o_ref.dtype)


def _attn_a_bias(rel_bias):
    qi = jnp.arange(A_Q_TILE)[:, None]
    kj = jnp.arange(A_WIN)[None, :]
    idx = jnp.clip(A_LEFT + qi - kj, -MAX_REL, MAX_REL) + MAX_REL
    bias = rel_bias.astype(F32)[:, idx]
    qc = qi // CHUNK
    kc = kj // CHUNK
    band = (kc >= qc) & (kc <= qc + LEFT_CHUNKS)
    return jnp.where(band[None], bias, NEG_INF)


def _attn_a(qkv, rel_bias):
    b, s, _ = qkv.shape
    nq = s // A_Q_TILE
    back = A_LEFT // A_Q_TILE
    blk = (1, A_Q_TILE, A_WIDTH)

    def kv_spec(col, off):
        return pl.BlockSpec(blk, lambda bi, i: (bi, jnp.maximum(i - back + off, 0), col))

    return pl.pallas_call(
        _attn_a_body,
        grid=(b, nq),
        in_specs=[pl.BlockSpec(blk, lambda bi, i: (bi, i, 0))]
        + [kv_spec(1, off) for off in range(back + 1)]
        + [kv_spec(2, off) for off in range(back + 1)]
        + [pl.BlockSpec((A_HEADS, A_Q_TILE, A_WIN), lambda bi, i: (0, 0, 0), pipeline_mode=pl.Buffered(1))],
        out_specs=pl.BlockSpec(blk, lambda bi, i: (bi, i, 0)),
        out_shape=jax.ShapeDtypeStruct((b, s, A_WIDTH), BF16),
        compiler_params=_cparams(("parallel", "arbitrary")),
        name="attn_a",
    )(qkv, qkv, qkv, qkv, qkv, qkv, qkv, _attn_a_bias(rel_bias))


def _attn_b_body(qn_ref, qp_ref, kn_ref, kpe_ref, v_ref, o_ref):
    qi = pl.program_id(2)
    q = jnp.concatenate([qn_ref[0], qp_ref[0]], axis=-1)
    shift = int(math.log2(CHUNK))
    q_row = lax.broadcasted_iota(jnp.int32, (B_TILE, B_TILE), 0)
    k_col = lax.broadcasted_iota(jnp.int32, (B_TILE, B_TILE), 1)
    chunk_gap = (q_row >> shift) - (k_col >> shift) + qi * (B_TILE // CHUNK)

    def step(j, carry):
        m_run, l_run, acc = carry
        rows = pl.ds(pl.multiple_of(j * B_TILE, B_TILE), B_TILE)
        k = jnp.concatenate([kn_ref[0, rows, :], kpe_ref[0, rows, :]], axis=-1)
        s = _dot_nt(q, k)
        s = jnp.where(chunk_gap >= j * (B_TILE // CHUNK), s, NEG_INF)
        m_new = jnp.maximum(m_run, jnp.max(s, axis=-1, keepdims=True))
        corr = jnp.exp(m_run - m_new)
        p = jnp.exp(s - m_new)
        l_new = corr * l_run + jnp.sum(p, axis=-1, keepdims=True)
        acc_new = corr * acc + _dot(p.astype(BF16), v_ref[0, rows, :])
        return m_new, l_new, acc_new

    init = (jnp.full((B_TILE, 1), NEG_INF, F32), jnp.zeros((B_TILE, 1), F32), jnp.zeros((B_TILE, V_HEAD), F32))
    _, l_fin, acc = lax.fori_loop(0, qi + 1, step, init)
    o_ref[0] = (acc / l_fin).astype(o_ref.dtype)


def _attn_b(q_nope, q_pe, kv, k_pe):
    b, s, _ = q_nope.shape
    qblk = (1, B_TILE, LANE)
    kblk = (1, s, LANE)
    return pl.pallas_call(
        _attn_b_body,
        grid=(b, B_HEADS, s // B_TILE),
        in_specs=[
            pl.BlockSpec(qblk, lambda bi, h, i: (bi, i, h)),
            pl.BlockSpec(qblk, lambda bi, h, i: (bi, i, h)),
            pl.BlockSpec(kblk, lambda bi, h, i: (bi, 0, h)),
            pl.BlockSpec(kblk, lambda bi, h, i: (bi, 0, 0)),
            pl.BlockSpec(kblk, lambda bi, h, i: (bi, 0, B_HEADS + h)),
        ],
        out_specs=pl.BlockSpec(qblk, lambda bi, h, i: (bi, i, h)),
        out_shape=jax.ShapeDtypeStruct((b, s, B_WIDTH), BF16),
        compiler_params=_cparams(("parallel", "parallel", "arbitrary")),
        name="attn_b",
    )(q_nope, q_pe, kv, k_pe, kv)


def _gate_mix_body(oa_ref, ob_ref, ga_ref, gb_ref, wa_ref, wb_ref, z_ref):
    ya = _dot(oa_ref[...], wa_ref[...])
    yb = _dot(ob_ref[...], wb_ref[...])
    z_ref[...] = (ga_ref[...].astype(F32) * ya + gb_ref[...].astype(F32) * yb).astype(z_ref.dtype)


def _gate_mix(o_a, o_b, gates, w_o_a, w_o_b):
    m = o_a.shape[0]
    tm = min(ROW_TILE, m)
    tn = 1024
    nj = D_MODEL // tn
    return pl.pallas_call(
        _gate_mix_body,
        grid=(m // tm, nj),
        in_specs=[
            pl.BlockSpec((tm, A_WIDTH), lambda i, j: (i, 0)),
            pl.BlockSpec((tm, B_WIDTH), lambda i, j: (i, 0)),
            pl.BlockSpec((tm, tn), lambda i, j: (i, j)),
            pl.BlockSpec((tm, tn), lambda i, j: (i, nj + j)),
            pl.BlockSpec((A_WIDTH, tn), lambda i, j: (0, j)),
            pl.BlockSpec((B_WIDTH, tn), lambda i, j: (0, j)),
        ],
        out_specs=pl.BlockSpec((tm, tn), lambda i, j: (i, j)),
        out_shape=jax.ShapeDtypeStruct((m, D_MODEL), BF16),
        compiler_params=_cparams(("parallel", "parallel")),
        name="gate_mix",
    )(o_a, o_b, gates, gates, w_o_a.astype(BF16), w_o_b.astype(BF16))


def _proj_res_ln_body(x_ref, w_ref, res_ref, g_ref, b_ref, o_ref):
    y = ALPHA * res_ref[...] + _dot(x_ref[...], w_ref[...])
    o_ref[...] = _layer_norm(y, g_ref[...], b_ref[...])


def _proj_res_ln(x, w, res, ln_g, ln_b):
    m, k = x.shape
    tm = min(ROW_TILE, m)
    row = lambda i: (i, 0)
    const = lambda i: (0, 0)
    return pl.pallas_call(
        _proj_res_ln_body,
        grid=(m // tm,),
        in_specs=[
            pl.BlockSpec((tm, k), row),
            pl.BlockSpec((k, D_MODEL), const),
            pl.BlockSpec((tm, D_MODEL), row),
            pl.BlockSpec((1, D_MODEL), const),
            pl.BlockSpec((1, D_MODEL), const),
        ],
        out_specs=pl.BlockSpec((tm, D_MODEL), row),
        out_shape=jax.ShapeDtypeStruct((m, D_MODEL), F32),
        compiler_params=_cparams(("parallel",)),
        name="proj_res_ln",
    )(x, w.astype(BF16), res, ln_g.reshape(1, -1), ln_b.reshape(1, -1))


def _xattn_body(q_ref, kv_ref, w_ref, res_ref, g_ref, b_ref, o_ref, att_ref):
    for h in range(X_HEADS):
        sl = slice(h * X_HEAD_DIM, (h + 1) * X_HEAD_DIM)
        vs = slice(D_MODEL + h * X_HEAD_DIM, D_MODEL + (h + 1) * X_HEAD_DIM)
        s = _dot_nt(q_ref[0, :, sl], kv_ref[0, :, sl])
        mx = jnp.max(s, axis=-1, keepdims=True)
        e = jnp.exp(s - mx)
        den = jnp.sum(e, axis=-1, keepdims=True)
        att_ref[:, sl] = (_dot(e.astype(BF16), kv_ref[0, :, vs]) / den).astype(BF16)
    y = ALPHA * res_ref[0] + _dot(att_ref[...], w_ref[...])
    o_ref[0] = _layer_norm(y, g_ref[...], b_ref[...])


def _xattn(q, kv, w_xo, res, ln_g, ln_b):
    b, s, _ = q.shape
    tm = min(ROW_TILE, s)
    m_len = kv.shape[1]
    row = lambda bi, i: (bi, i, 0)
    return pl.pallas_call(
        _xattn_body,
        grid=(b, s // tm),
        in_specs=[
            pl.BlockSpec((1, tm, D_MODEL), row),
            pl.BlockSpec((1, m_len, 2 * D_MODEL), lambda bi, i: (bi, 0, 0)),
            pl.BlockSpec((D_MODEL, D_MODEL), lambda bi, i: (0, 0)),
            pl.BlockSpec((1, tm, D_MODEL), row),
            pl.BlockSpec((1, D_MODEL), lambda bi, i: (0, 0)),
            pl.BlockSpec((1, D_MODEL), lambda bi, i: (0, 0)),
        ],
        out_specs=pl.BlockSpec((1, tm, D_MODEL), row),
        out_shape=jax.ShapeDtypeStruct((b, s, D_MODEL), F32),
        scratch_shapes=[pltpu.VMEM((tm, D_MODEL), BF16)],
        compiler_params=_cparams(("parallel", "parallel")),
        name="xattn",
    )(q, kv, w_xo.astype(BF16), res, ln_g.reshape(1, -1), ln_b.reshape(1, -1))


def _in_proj_weights(w_in):
    qkv_end = 3 * A_WIDTH
    lat_end = qkv_end + Q_LORA + KV_LORA
    half = QK_ROPE // 2
    col_scale = jnp.concatenate([jnp.full((A_WIDTH,), A_HEAD_DIM ** -0.5, F32), jnp.ones((2 * A_WIDTH,), F32)])
    w_qkv = (w_in[:, :qkv_end] * col_scale).astype(BF16)
    x1 = w_in[:, lat_end:lat_end + half]
    x2 = w_in[:, lat_end + half:lat_end + QK_ROPE]
    kpe_a, kpe_b = _rot_pair(x1, x2, (D_MODEL, LANE - QK_ROPE))
    w_lat = jnp.concatenate([w_in[:, qkv_end:lat_end], kpe_a, kpe_b], axis=1).astype(BF16)
    w_gate = w_in[:, lat_end + QK_ROPE:].astype(BF16)
    return w_qkv, w_lat, w_gate


def kernel(x, mem, ffn1_w_in, ffn1_w_out, ln_ffn1_g, ln_ffn1_b, w_in, gate_bias, rel_bias, q_a_norm, w_q_b,
           kv_a_norm, w_kv_b, w_o_a, w_o_b, w_out, ln_mix_g, ln_mix_b, mem_ln_g, mem_ln_b, w_xq, w_xkv, w_xo,
           ln_x_g, ln_x_b, ffn2_w_in, ffn2_w_out, ln_ffn2_g, ln_ffn2_b):
    b, s, d = x.shape
    m = b * s
    h = x.reshape(m, d)
    for l in range(DEPTH):
        h = _ffn_ln(h, ffn1_w_in[l], ffn1_w_out[l], ln_ffn1_g[l], ln_ffn1_b[l])

        w_qkv, w_lat, w_gate = _in_proj_weights(w_in[l])
        qkv = _matmul(h, w_qkv, out_dtype=BF16, tn=1024, name="in_proj_qkv")
        gates = _matmul(h, w_gate, out_dtype=BF16, tn=1024, bias=gate_bias[l], act="sigmoid", name="in_proj_gates")
        lat = _matmul(h, w_lat, out_dtype=F32, tn=LAT_COLS, name="in_proj_latent")
        o_a = _attn_a(qkv.reshape(b, s, 3 * A_WIDTH), rel_bias[l])
        q_nope, q_pe, kv, k_pe = _latent_up(lat, q_a_norm[l], w_q_b[l], kv_a_norm[l], w_kv_b[l], s)
        o_b = _attn_b(q_nope.reshape(b, s, B_WIDTH), q_pe.reshape(b, s, B_WIDTH),
                      kv.reshape(b, s, 2 * B_WIDTH), k_pe.reshape(b, s, LANE))
        z = _gate_mix(o_a.reshape(m, A_WIDTH), o_b.reshape(m, B_WIDTH), gates, w_o_a[l], w_o_b[l])
        h = _proj_res_ln(z, w_out[l], h, ln_mix_g[l], ln_mix_b[l])

        m_len = mem.shape[1]
        kv_x = _matmul(mem.reshape(b * m_len, d), w_xkv[l].astype(BF16), out_dtype=BF16, tn=1024,
                       pro="ln", pro_g=mem_ln_g[l], pro_b=mem_ln_b[l], name="mem_kv")
        q_x = _matmul(h, w_xq[l].astype(BF16), out_dtype=BF16, tn=1024, out_scale=X_HEAD_DIM ** -0.5, name="xattn_q")
        h = _xattn(q_x.reshape(b, s, d), kv_x.reshape(b, m_len, 2 * d), w_xo[l], h.reshape(b, s, d),
                   ln_x_g[l], ln_x_b[l]).reshape(m, d)

        h = _ffn_ln(h, ffn2_w_in[l], ffn2_w_out[l], ln_ffn2_g[l], ln_ffn2_b[l])
    return h.reshape(b, s, d)
```

```python
import functools
import math

import jax
import jax.numpy as jnp
from jax import lax
from jax.experimental import pallas as pl
from jax.experimental.pallas import tpu as pltpu

BF16 = jnp.bfloat16
F32 = jnp.float32

D_MODEL = 2048
CHUNK = 64
LEFT_CHUNKS = 8
MEM_LEN = 256
A_HEADS = 16
A_HEAD_DIM = 64
A_WIDTH = A_HEADS * A_HEAD_DIM
MAX_REL = 128
B_HEADS = 16
QK_NOPE = 128
QK_ROPE = 64
V_HEAD = 128
Q_LORA = 512
KV_LORA = 512
ROPE_THETA = 10000.0
B_WIDTH = B_HEADS * V_HEAD
X_HEADS = 4
X_HEAD_DIM = D_MODEL // X_HEADS
D_FF = 5504
DEPTH = 1
ALPHA = (2.0 * DEPTH) ** 0.25
LN_EPS = 1e-5
RMS_EPS = 1e-6
NEG_INF = -1e30

LANE = 128
VMEM_LIMIT_BYTES = 56 * 1024 * 1024

FF_TILE = 512
D_FF_PAD = ((D_FF + FF_TILE - 1) // FF_TILE) * FF_TILE
ROW_TILE = 512
A_Q_TILE = 256
A_LEFT = LEFT_CHUNKS * CHUNK
A_WIN = A_LEFT + A_Q_TILE
A_TOEP = A_Q_TILE + A_WIN
B_TILE = 512
B_GROUP = 4
B_VT_ROWS = V_HEAD + 16
LAT_COLS = Q_LORA + KV_LORA + 2 * LANE


def _cparams(semantics):
    return pltpu.CompilerParams(dimension_semantics=semantics, vmem_limit_bytes=VMEM_LIMIT_BYTES)


def _layer_norm(x, g, b):
    mu = jnp.mean(x, axis=-1, keepdims=True)
    xc = x - mu
    var = jnp.mean(xc * xc, axis=-1, keepdims=True)
    return xc * lax.rsqrt(var + LN_EPS) * g + b


def _rms_norm(x, g):
    return x * lax.rsqrt(jnp.mean(x * x, axis=-1, keepdims=True) + RMS_EPS) * g


def _dot(a, b):
    return jnp.dot(a, b, preferred_element_type=F32)


def _dot_nt(a, b):
    return lax.dot_general(a, b, (((1,), (1,)), ((), ())), preferred_element_type=F32)


def _ffn_body(x_ref, wgu_ref, wo_ref, g_ref, b_ref, o_ref, xb_ref, acc_ref):
    j = pl.program_id(1)

    @pl.when(j == 0)
    def _():
        xb_ref[...] = x_ref[...].astype(BF16)
        acc_ref[...] = jnp.zeros_like(acc_ref)

    gu = _dot(xb_ref[...], wgu_ref[...])
    gate = gu[:, :FF_TILE]
    up = gu[:, FF_TILE:]
    act = (gate * jax.nn.sigmoid(gate)) * up
    acc_ref[...] += _dot(act.astype(BF16), wo_ref[...])

    @pl.when(j == pl.num_programs(1) - 1)
    def _():
        y = ALPHA * x_ref[...] + 0.5 * acc_ref[...]
        o_ref[...] = _layer_norm(y, g_ref[...], b_ref[...])


def _ffn_weights(w_in, w_out):
    pad = D_FF_PAD - D_FF
    n_tiles = D_FF_PAD // FF_TILE
    wg = jnp.pad(w_in[:, :D_FF].astype(BF16), ((0, 0), (0, pad))).reshape(D_MODEL, n_tiles, FF_TILE)
    wu = jnp.pad(w_in[:, D_FF:].astype(BF16), ((0, 0), (0, pad))).reshape(D_MODEL, n_tiles, FF_TILE)
    wgu = jnp.concatenate([wg, wu], axis=2).reshape(D_MODEL, 2 * D_FF_PAD)
    wo = jnp.pad(w_out.astype(BF16), ((0, pad), (0, 0)))
    return wgu, wo


def _ffn_ln(x, w_in, w_out, ln_g, ln_b):
    m = x.shape[0]
    tm = min(ROW_TILE, m)
    wgu, wo = _ffn_weights(w_in, w_out)
    return pl.pallas_call(
        _ffn_body,
        grid=(m // tm, D_FF_PAD // FF_TILE),
        in_specs=[
            pl.BlockSpec((tm, D_MODEL), lambda i, j: (i, 0)),
            pl.BlockSpec((D_MODEL, 2 * FF_TILE), lambda i, j: (0, j)),
            pl.BlockSpec((FF_TILE, D_MODEL), lambda i, j: (j, 0)),
            pl.BlockSpec((1, D_MODEL), lambda i, j: (0, 0)),
            pl.BlockSpec((1, D_MODEL), lambda i, j: (0, 0)),
        ],
        out_specs=pl.BlockSpec((tm, D_MODEL), lambda i, j: (i, 0)),
        out_shape=jax.ShapeDtypeStruct((m, D_MODEL), F32),
        scratch_shapes=[pltpu.VMEM((tm, D_MODEL), BF16), pltpu.VMEM((tm, D_MODEL), F32)],
        compiler_params=_cparams(("parallel", "arbitrary")),
        name="ffn_ln",
    )(x, wgu, wo, ln_g.reshape(1, -1), ln_b.reshape(1, -1))


def _mm_body(*refs, pro, has_bias, act, out_scale):
    it = iter(refs)
    x_ref = next(it)
    pg_ref = next(it) if pro else None
    pb_ref = next(it) if pro == "ln" else None
    w_ref = next(it)
    bias_ref = next(it) if has_bias else None
    o_ref = next(it)
    x = x_ref[...]
    if pro == "ln":
        x = _layer_norm(x.astype(F32), pg_ref[...], pb_ref[...])
    elif pro == "rms":
        x = _rms_norm(x.astype(F32), pg_ref[...])
    y = _dot(x.astype(BF16), w_ref[...])
    if has_bias:
        y = y + bias_ref[...]
    if act == "sigmoid":
        y = jax.nn.sigmoid(y)
    if out_scale is not None:
        y = y * out_scale
    o_ref[...] = y.astype(o_ref.dtype)


def _matmul(x, w, *, out_dtype, tn, x_cols=None, pro=None, pro_g=None, pro_b=None,
            bias=None, act=None, out_scale=None, name="matmul"):
    m = x.shape[0]
    k, n = w.shape
    tm = min(ROW_TILE, m)
    xblk = 0 if x_cols is None else x_cols
    args = [x]
    specs = [pl.BlockSpec((tm, k), lambda i, j: (i, xblk))]
    if pro:
        args.append(pro_g.reshape(1, k))
        specs.append(pl.BlockSpec((1, k), lambda i, j: (0, 0)))
    if pro == "ln":
        args.append(pro_b.reshape(1, k))
        specs.append(pl.BlockSpec((1, k), lambda i, j: (0, 0)))
    args.append(w)
    specs.append(pl.BlockSpec((k, tn), lambda i, j: (0, j)))
    if bias is not None:
        args.append(bias.reshape(1, n))
        specs.append(pl.BlockSpec((1, tn), lambda i, j: (0, j)))
    body = functools.partial(_mm_body, pro=pro, has_bias=bias is not None, act=act, out_scale=out_scale)
    return pl.pallas_call(
        body,
        grid=(m // tm, n // tn),
        in_specs=specs,
        out_specs=pl.BlockSpec((tm, tn), lambda i, j: (i, j)),
        out_shape=jax.ShapeDtypeStruct((m, n), out_dtype),
        compiler_params=_cparams(("parallel", "parallel")),
        name=name,
    )(*args)


def _q_up_body(lat_ref, g_ref, w_ref, cos_ref, sin_ref, qn_ref, qp_ref, *, scale):
    c = _rms_norm(lat_ref[...], g_ref[...]).astype(BF16)
    qn_ref[...] = (_dot(c, w_ref[:, :B_WIDTH]) * scale).astype(BF16)
    pa = _dot(c, w_ref[:, B_WIDTH:2 * B_WIDTH])
    pb = _dot(c, w_ref[:, 2 * B_WIDTH:])
    cos = cos_ref[...] * scale
    sin = sin_ref[...] * scale
    for h in range(B_HEADS):
        sl = slice(h * LANE, (h + 1) * LANE)
        qp_ref[:, sl] = (pa[:, sl] * cos + pb[:, sl] * sin).astype(BF16)


def _kv_up_body(c_ref, g_ref, wk_ref, wvt_ref, ka_ref, kb_ref, cos_ref, sin_ref, kn_ref, vt_ref, kpe_ref):
    c = _rms_norm(c_ref[...], g_ref[...]).astype(BF16)
    n_chunk = 1024
    for n in range(B_WIDTH // n_chunk):
        sl = slice(n * n_chunk, (n + 1) * n_chunk)
        kn_ref[:, sl] = _dot(c, wk_ref[:, sl]).astype(BF16)
    heads_per_dot = 4
    tm = c.shape[0]
    ones_rows = (lax.broadcasted_iota(jnp.int32, (B_VT_ROWS - V_HEAD, tm), 0) == 0).astype(BF16)
    for hc in range(B_HEADS // heads_per_dot):
        rows = slice(hc * heads_per_dot * V_HEAD, (hc + 1) * heads_per_dot * V_HEAD)
        vt = _dot_nt(wvt_ref[rows, :], c).astype(BF16)
        for r in range(heads_per_dot):
            vt_ref[0, hc * heads_per_dot + r, 0, :V_HEAD, :] = vt[r * V_HEAD:(r + 1) * V_HEAD, :]
            vt_ref[0, hc * heads_per_dot + r, 0, V_HEAD:, :] = ones_rows
    kpe_ref[...] = (ka_ref[...] * cos_ref[...] + kb_ref[...] * sin_ref[...]).astype(BF16)


def _rope_tables(seq_len):
    half = QK_ROPE // 2
    inv = 1.0 / (ROPE_THETA ** (jnp.arange(0, QK_ROPE, 2, dtype=F32) / QK_ROPE))
    ang = jnp.arange(seq_len, dtype=F32)[:, None] * inv[None, :]
    zeros = jnp.zeros((seq_len, LANE - 2 * half), F32)
    cos = jnp.concatenate([jnp.cos(ang), jnp.cos(ang), zeros], axis=1)
    sin = jnp.concatenate([jnp.sin(ang), jnp.sin(ang), zeros], axis=1)
    return cos, sin


def _rot_pair(x1, x2, pad_shape):
    zeros = jnp.zeros(pad_shape, x1.dtype)
    return (jnp.concatenate([x1, x2, zeros], axis=-1), jnp.concatenate([-x2, x1, zeros], axis=-1))


def _latent_up(lat, q_a_norm, w_q_b, kv_a_norm, w_kv_b, seq_len):
    m = lat.shape[0]
    tm = min(B_TILE, seq_len)
    n_pos = seq_len // tm
    cos, sin = _rope_tables(seq_len)
    half = QK_ROPE // 2
    wq = w_q_b.reshape(Q_LORA, B_HEADS, QK_NOPE + QK_ROPE)
    pe = wq[:, :, QK_NOPE:]
    pe_a, pe_b = _rot_pair(pe[..., :half], pe[..., half:], (Q_LORA, B_HEADS, LANE - QK_ROPE))
    w_q_all = jnp.concatenate([wq[:, :, :QK_NOPE].reshape(Q_LORA, B_WIDTH),
                               pe_a.reshape(Q_LORA, B_WIDTH), pe_b.reshape(Q_LORA, B_WIDTH)],
                              axis=1).astype(BF16)
    wkv = w_kv_b.reshape(KV_LORA, B_HEADS, QK_NOPE + V_HEAD)
    w_k = wkv[:, :, :QK_NOPE].reshape(KV_LORA, B_WIDTH).astype(BF16)
    w_vt = wkv[:, :, QK_NOPE:].reshape(KV_LORA, B_WIDTH).T.astype(BF16)
    scale = (QK_NOPE + QK_ROPE) ** -0.5 * math.log2(math.e)
    row = lambda i: (i, 0)
    const = lambda i: (0, 0)
    pos = lambda i: (i % n_pos, 0)
    q_nope, q_pe = pl.pallas_call(
        functools.partial(_q_up_body, scale=scale),
        grid=(m // tm,),
        in_specs=[
            pl.BlockSpec((tm, Q_LORA), row),
            pl.BlockSpec((1, Q_LORA), const),
            pl.BlockSpec((Q_LORA, 3 * B_WIDTH), const),
            pl.BlockSpec((tm, LANE), pos),
            pl.BlockSpec((tm, LANE), pos),
        ],
        out_specs=[pl.BlockSpec((tm, B_WIDTH), row), pl.BlockSpec((tm, B_WIDTH), row)],
        out_shape=[jax.ShapeDtypeStruct((m, B_WIDTH), BF16), jax.ShapeDtypeStruct((m, B_WIDTH), BF16)],
        compiler_params=_cparams(("parallel",)),
        name="q_up",
    )(lat, q_a_norm.reshape(1, -1), w_q_all, cos, sin)
    ka_blk = (Q_LORA + KV_LORA) // LANE
    batch = m // seq_len
    k_nope, v_t, k_pe = pl.pallas_call(
        _kv_up_body,
        grid=(m // tm,),
        in_specs=[
            pl.BlockSpec((tm, KV_LORA), lambda i: (i, Q_LORA // KV_LORA)),
            pl.BlockSpec((1, KV_LORA), const),
            pl.BlockSpec((KV_LORA, B_WIDTH), const),
            pl.BlockSpec((B_WIDTH, KV_LORA), const),
            pl.BlockSpec((tm, LANE), lambda i: (i, ka_blk)),
            pl.BlockSpec((tm, LANE), lambda i: (i, ka_blk + 1)),
            pl.BlockSpec((tm, LANE), pos),
            pl.BlockSpec((tm, LANE), pos),
        ],
        out_specs=[pl.BlockSpec((tm, B_WIDTH), row),
                   pl.BlockSpec((1, B_HEADS, 1, B_VT_ROWS, tm), lambda i: (i // n_pos, 0, i % n_pos, 0, 0)),
                   pl.BlockSpec((tm, LANE), row)],
        out_shape=[jax.ShapeDtypeStruct((m, B_WIDTH), BF16),
                   jax.ShapeDtypeStruct((batch, B_HEADS, n_pos, B_VT_ROWS, tm), BF16),
                   jax.ShapeDtypeStruct((m, LANE), BF16)],
        compiler_params=_cparams(("parallel",)),
        name="kv_up",
    )(lat, kv_a_norm.reshape(1, -1), w_k, w_vt, lat, lat, cos, sin)
    return q_nope, q_pe, k_nope, v_t, k_pe


def _attn_a_body(q_ref, k0_ref, k1_ref, k2_ref, v0_ref, v1_ref, v2_ref, toep_ref, o_ref, bias_ref):
    i = pl.program_id(1)

    @pl.when((pl.program_id(0) == 0) & (i == 0))
    def _():
        shift = int(math.log2(CHUNK))
        qc = lax.broadcasted_iota(jnp.int32, (A_Q_TILE, A_WIN), 0) >> shift
        kc = lax.broadcasted_iota(jnp.int32, (A_Q_TILE, A_WIN), 1) >> shift
        band = (kc >= qc) & (kc <= qc + LEFT_CHUNKS)
        for h in range(A_HEADS):
            rows = jnp.broadcast_to(toep_ref[h:h + 1, :], (A_Q_TILE, A_TOEP))
            tile = pltpu.roll(rows, 0, 1, stride=1, stride_axis=0)
            bias_ref[h] = jnp.where(band, tile[:, :A_WIN], NEG_INF)

    col = lax.broadcasted_iota(jnp.int32, (A_Q_TILE, A_WIN), 1)
    in_seq = col >= (A_LEFT // A_Q_TILE - i) * A_Q_TILE
    lane = lax.broadcasted_iota(jnp.int32, (1, LANE), 1)
    low = lane < A_HEAD_DIM
    for hp in range(A_WIDTH // LANE):
        sl = slice(hp * LANE, (hp + 1) * LANE)
        qp = q_ref[0, :, sl]
        kp = jnp.concatenate([k0_ref[0, :, sl], k1_ref[0, :, sl], k2_ref[0, :, sl]], axis=0)
        vp = jnp.concatenate([v0_ref[0, :, sl], v1_ref[0, :, sl], v2_ref[0, :, sl]], axis=0)
        outs = []
        for sub in range(2):
            sel = low if sub == 0 else jnp.logical_not(low)
            qm = jnp.where(sel, qp, jnp.zeros_like(qp))
            s = _dot_nt(qm, kp)
            s = jnp.where(in_seq, s + bias_ref[2 * hp + sub], NEG_INF)
            mx = jnp.max(s, axis=-1, keepdims=True)
            e = jnp.exp(s - mx)
            den = jnp.sum(e, axis=-1, keepdims=True)
            outs.append(_dot(e.astype(BF16), vp) / den)
        o_ref[0, :, sl] = jnp.where(low, outs[0], outs[1]).astype(o_ref.dtype)


def _attn_a_toeplitz_row(rel_bias):
    n_head = rel_bias.shape[0]
    far_left = jnp.broadcast_to(rel_bias[:, -1:], (n_head, A_LEFT - MAX_REL))
    far_right = jnp.broadcast_to(rel_bias[:, :1], (n_head, A_WIN - A_LEFT - MAX_REL))
    wrapped = jnp.broadcast_to(rel_bias[:, -1:], (n_head, A_Q_TILE - 1))
    return jnp.concatenate([far_left, rel_bias[:, ::-1], far_right, wrapped], axis=1).astype(F32)


def _attn_a(qkv, rel_bias):
    b, s, _ = qkv.shape
    nq = s // A_Q_TILE
    back = A_LEFT // A_Q_TILE
    blk = (1, A_Q_TILE, A_WIDTH)

    def kv_spec(col, off):
        return pl.BlockSpec(blk, lambda bi, i: (bi, jnp.maximum(i - back + off, 0), col))

    return pl.pallas_call(
        _attn_a_body,
        grid=(b, nq),
        in_specs=[pl.BlockSpec(blk, lambda bi, i: (bi, i, 0))]
        + [kv_spec(1, off) for off in range(back + 1)]
        + [kv_spec(2, off) for off in range(back + 1)]
        + [pl.BlockSpec((A_HEADS, A_TOEP), lambda bi, i: (0, 0))],
        out_specs=pl.BlockSpec(blk, lambda bi, i: (bi, i, 0)),
        out_shape=jax.ShapeDtypeStruct((b, s, A_WIDTH), BF16),
        scratch_shapes=[pltpu.VMEM((A_HEADS, A_Q_TILE, A_WIN), F32)],
        compiler_params=_cparams(("arbitrary", "arbitrary")),
        name="attn_a",
    )(qkv, qkv, qkv, qkv, qkv, qkv, qkv, _attn_a_toeplitz_row(rel_bias))


def _attn_b_body(qn_ref, qp_ref, kn_ref, kpe_ref, vt_ref, o_ref, m_ref, acc_ref):
    qi = pl.program_id(2)
    t = qn_ref.shape[1]
    slabs = [slice(g * LANE, (g + 1) * LANE) for g in range(B_GROUP)]
    qs = [jnp.concatenate([qn_ref[0, :, sl], qp_ref[0, :, sl]], axis=-1) for sl in slabs]
    shift = int(math.log2(CHUNK))
    key_chunk = lax.broadcasted_iota(jnp.int32, (t, t), 0) >> shift
    qry_chunk = lax.broadcasted_iota(jnp.int32, (t, t), 1) >> shift
    diag_allowed = key_chunk <= qry_chunk

    m_ref[...] = jnp.full_like(m_ref, NEG_INF)
    acc_ref[...] = jnp.zeros_like(acc_ref)

    def step(j, masked):
        rows = pl.ds(pl.multiple_of(j * t, t), t)
        kpe = kpe_ref[0, rows, :]
        scores = [_dot_nt(jnp.concatenate([kn_ref[0, rows, sl], kpe], axis=-1), qs[g])
                  for g, sl in enumerate(slabs)]
        for g in range(B_GROUP):
            st = scores[g]
            if masked:
                st = jnp.where(diag_allowed, st, NEG_INF)
            m_old = m_ref[g]
            m_new = jnp.maximum(m_old, jnp.max(st, axis=0, keepdims=True))
            corr = jnp.exp2(m_old - m_new)
            p = jnp.exp2(st - m_new).astype(BF16)
            m_ref[g] = m_new
            acc_ref[g] = corr * acc_ref[g] + _dot(vt_ref[0, g, j], p)

    def full_step(j, carry):
        step(j, False)
        return carry

    lax.fori_loop(0, qi, full_step, 0)
    step(qi, True)
    for g, sl in enumerate(slabs):
        o_ref[0, :, sl] = (acc_ref[g, :V_HEAD, :] / acc_ref[g, V_HEAD:V_HEAD + 1, :]).T.astype(o_ref.dtype)


def _attn_b(q_nope, q_pe, k_nope, k_pe, v_t):
    b, s, _ = q_nope.shape
    t = v_t.shape[-1]
    width = B_GROUP * LANE
    qblk = (1, t, width)
    return pl.pallas_call(
        _attn_b_body,
        grid=(b, B_HEADS // B_GROUP, s // t),
        in_specs=[
            pl.BlockSpec(qblk, lambda bi, hg, i: (bi, i, hg)),
            pl.BlockSpec(qblk, lambda bi, hg, i: (bi, i, hg)),
            pl.BlockSpec((1, s, width), lambda bi, hg, i: (bi, 0, hg)),
            pl.BlockSpec((1, s, LANE), lambda bi, hg, i: (bi, 0, 0)),
            pl.BlockSpec((1, B_GROUP, s // t, B_VT_ROWS, t), lambda bi, hg, i: (bi, hg, 0, 0, 0)),
        ],
        out_specs=pl.BlockSpec(qblk, lambda bi, hg, i: (bi, i, hg)),
        out_shape=jax.ShapeDtypeStruct((b, s, B_WIDTH), BF16),
        scratch_shapes=[pltpu.VMEM((B_GROUP, 1, t), F32), pltpu.VMEM((B_GROUP, B_VT_ROWS, t), F32)],
        compiler_params=_cparams(("parallel", "parallel", "arbitrary")),
        name="attn_b",
    )(q_nope, q_pe, k_nope, k_pe, v_t)


def _gate_mix_body(oa_ref, ob_ref, ga_ref, gb_ref, wa_ref, wb_ref, z_ref):
    ya = _dot(oa_ref[...], wa_ref[...])
    yb = _dot(ob_ref[...], wb_ref[...])
    z_ref[...] = (ga_ref[...].astype(F32) * ya + gb_ref[...].astype(F32) * yb).astype(z_ref.dtype)


def _gate_mix(o_a, o_b, gates, w_o_a, w_o_b):
    m = o_a.shape[0]
    tm = min(ROW_TILE, m)
    tn = 1024
    nj = D_MODEL // tn
    return pl.pallas_call(
        _gate_mix_body,
        grid=(m // tm, nj),
        in_specs=[
            pl.BlockSpec((tm, A_WIDTH), lambda i, j: (i, 0)),
            pl.BlockSpec((tm, B_WIDTH), lambda i, j: (i, 0)),
            pl.BlockSpec((tm, tn), lambda i, j: (i, j)),
            pl.BlockSpec((tm, tn), lambda i, j: (i, nj + j)),
            pl.BlockSpec((A_WIDTH, tn), lambda i, j: (0, j)),
            pl.BlockSpec((B_WIDTH, tn), lambda i, j: (0, j)),
        ],
        out_specs=pl.BlockSpec((tm, tn), lambda i, j: (i, j)),
        out_shape=jax.ShapeDtypeStruct((m, D_MODEL), BF16),
        compiler_params=_cparams(("parallel", "parallel")),
        name="gate_mix",
    )(o_a, o_b, gates, gates, w_o_a.astype(BF16), w_o_b.astype(BF16))


def _proj_res_ln_body(x_ref, w_ref, res_ref, g_ref, b_ref, o_ref):
    y = ALPHA * res_ref[...] + _dot(x_ref[...], w_ref[...])
    o_ref[...] = _layer_norm(y, g_ref[...], b_ref[...])


def _proj_res_ln(x, w, res, ln_g, ln_b):
    m, k = x.shape
    tm = min(ROW_TILE, m)
    row = lambda i: (i, 0)
    const = lambda i: (0, 0)
    return pl.pallas_call(
        _proj_res_ln_body,
        grid=(m // tm,),
        in_specs=[
            pl.BlockSpec((tm, k), row),
            pl.BlockSpec((k, D_MODEL), const),
            pl.BlockSpec((tm, D_MODEL), row),
            pl.BlockSpec((1, D_MODEL), const),
            pl.BlockSpec((1, D_MODEL), const),
        ],
        out_specs=pl.BlockSpec((tm, D_MODEL), row),
        out_shape=jax.ShapeDtypeStruct((m, D_MODEL), F32),
        compiler_params=_cparams(("parallel",)),
        name="proj_res_ln",
    )(x, w.astype(BF16), res, ln_g.reshape(1, -1), ln_b.reshape(1, -1))


def _xattn_body(q_ref, kv_ref, w_ref, res_ref, g_ref, b_ref, o_ref, att_ref):
    for h in range(X_HEADS):
        sl = slice(h * X_HEAD_DIM, (h + 1) * X_HEAD_DIM)
        vs = slice(D_MODEL + h * X_HEAD_DIM, D_MODEL + (h + 1) * X_HEAD_DIM)
        s = _dot_nt(q_ref[0, :, sl], kv_ref[0, :, sl])
        mx = jnp.max(s, axis=-1, keepdims=True)
        e = jnp.exp(s - mx)
        den = jnp.sum(e, axis=-1, keepdims=True)
        att_ref[:, sl] = (_dot(e.astype(BF16), kv_ref[0, :, vs]) / den).astype(BF16)
    y = ALPHA * res_ref[0] + _dot(att_ref[...], w_ref[...])
    o_ref[0] = _layer_norm(y, g_ref[...], b_ref[...])


def _xattn(q, kv, w_xo, res, ln_g, ln_b):
    b, s, _ = q.shape
    tm = min(ROW_TILE, s)
    m_len = kv.shape[1]
    row = lambda bi, i: (bi, i, 0)
    return pl.pallas_call(
        _xattn_body,
        grid=(b, s // tm),
        in_specs=[
            pl.BlockSpec((1, tm, D_MODEL), row),
            pl.BlockSpec((1, m_len, 2 * D_MODEL), lambda bi, i: (bi, 0, 0)),
            pl.BlockSpec((D_MODEL, D_MODEL), lambda bi, i: (0, 0)),
            pl.BlockSpec((1, tm, D_MODEL), row),
            pl.BlockSpec((1, D_MODEL), lambda bi, i: (0, 0)),
            pl.BlockSpec((1, D_MODEL), lambda bi, i: (0, 0)),
        ],
        out_specs=pl.BlockSpec((1, tm, D_MODEL), row),
        out_shape=jax.ShapeDtypeStruct((b, s, D_MODEL), F32),
        scratch_shapes=[pltpu.VMEM((tm, D_MODEL), BF16)],
        compiler_params=_cparams(("parallel", "parallel")),
        name="xattn",
    )(q, kv, w_xo.astype(BF16), res, ln_g.reshape(1, -1), ln_b.reshape(1, -1))


def _in_proj_weights(w_in):
    qkv_end = 3 * A_WIDTH
    lat_end = qkv_end + Q_LORA + KV_LORA
    half = QK_ROPE // 2
    col_scale = jnp.concatenate([jnp.full((A_WIDTH,), A_HEAD_DIM ** -0.5, F32), jnp.ones((2 * A_WIDTH,), F32)])
    w_qkv = (w_in[:, :qkv_end] * col_scale).astype(BF16)
    x1 = w_in[:, lat_end:lat_end + half]
    x2 = w_in[:, lat_end + half:lat_end + QK_ROPE]
    kpe_a, kpe_b = _rot_pair(x1, x2, (D_MODEL, LANE - QK_ROPE))
    w_lat = jnp.concatenate([w_in[:, qkv_end:lat_end], kpe_a, kpe_b], axis=1).astype(BF16)
    w_gate = w_in[:, lat_end + QK_ROPE:].astype(BF16)
    return w_qkv, w_lat, w_gate


def kernel(x, mem, ffn1_w_in, ffn1_w_out, ln_ffn1_g, ln_ffn1_b, w_in, gate_bias, rel_bias, q_a_norm, w_q_b,
           kv_a_norm, w_kv_b, w_o_a, w_o_b, w_out, ln_mix_g, ln_mix_b, mem_ln_g, mem_ln_b, w_xq, w_xkv, w_xo,
           ln_x_g, ln_x_b, ffn2_w_in, ffn2_w_out, ln_ffn2_g, ln_ffn2_b):
    b, s, d = x.shape
    m = b * s
    h = x.reshape(m, d)
    for l in range(DEPTH):
        h = _ffn_ln(h, ffn1_w_in[l], ffn1_w_out[l], ln_ffn1_g[l], ln_ffn1_b[l])

        w_qkv, w_lat, w_gate = _in_proj_weights(w_in[l])
        qkv = _matmul(h, w_qkv, out_dtype=BF16, tn=1024, name="in_proj_qkv")
        gates = _matmul(h, w_gate, out_dtype=BF16, tn=1024, bias=gate_bias[l], act="sigmoid", name="in_proj_gates")
        lat = _matmul(h, w_lat, out_dtype=F32, tn=LAT_COLS, name="in_proj_latent")
        o_a = _attn_a(qkv.reshape(b, s, 3 * A_WIDTH), rel_bias[l])
        q_nope, q_pe, k_nope, v_t, k_pe = _latent_up(lat, q_a_norm[l], w_q_b[l], kv_a_norm[l], w_kv_b[l], s)
        o_b = _attn_b(q_nope.reshape(b, s, B_WIDTH), q_pe.reshape(b, s, B_WIDTH),
                      k_nope.reshape(b, s, B_WIDTH), k_pe.reshape(b, s, LANE), v_t)
        z = _gate_mix(o_a.reshape(m, A_WIDTH), o_b.reshape(m, B_WIDTH), gates, w_o_a[l], w_o_b[l])
        h = _proj_res_ln(z, w_out[l], h, ln_mix_g[l], ln_mix_b[l])

        m_len = mem.shape[1]
        kv_x = _matmul(mem.reshape(b * m_len, d), w_xkv[l].astype(BF16), out_dtype=BF16, tn=1024,
                       pro="ln", pro_g=mem_ln_g[l], pro_b=mem_ln_b[l], name="mem_kv")
        q_x = _matmul(h, w_xq[l].astype(BF16), out_dtype=BF16, tn=1024, out_scale=X_HEAD_DIM ** -0.5, name="xattn_q")
        h = _xattn(q_x.reshape(b, s, d), kv_x.reshape(b, m_len, 2 * d), w_xo[l], h.reshape(b, s, d),
                   ln_x_g[l], ln_x_b[l]).reshape(m, d)

        h = _ffn_ln(h, ffn2_w_in[l], ffn2_w_out[l], ln_ffn2_g[l], ln_ffn2_b[l])
    return h.reshape(b, s, d)
```

```python
import functools
import math

import jax
import jax.numpy as jnp
from jax import lax
from jax.experimental import pallas as pl
from jax.experimental.pallas import tpu as pltpu

BF16 = jnp.bfloat16
F32 = jnp.float32

D_MODEL = 2048
CHUNK = 64
LEFT_CHUNKS = 8
MEM_LEN = 256
A_HEADS = 16
A_HEAD_DIM = 64
A_WIDTH = A_HEADS * A_HEAD_DIM
MAX_REL = 128
B_HEADS = 16
QK_NOPE = 128
QK_ROPE = 64
V_HEAD = 128
Q_LORA = 512
KV_LORA = 512
ROPE_THETA = 10000.0
B_WIDTH = B_HEADS * V_HEAD
X_HEADS = 4
X_HEAD_DIM = D_MODEL // X_HEADS
D_FF = 5504
DEPTH = 1
ALPHA = (2.0 * DEPTH) ** 0.25
LN_EPS = 1e-5
RMS_EPS = 1e-6
NEG_INF = -1e30

LANE = 128
VMEM_LIMIT_BYTES = 56 * 1024 * 1024

FF_TILE = 512
D_FF_PAD = ((D_FF + FF_TILE - 1) // FF_TILE) * FF_TILE
ROW_TILE = 512
FFN_ROW_TILE = 1024
A_Q_TILE = 256
A_LEFT = LEFT_CHUNKS * CHUNK
A_WIN = A_LEFT + A_Q_TILE
A_TOEP = A_Q_TILE + A_WIN
B_TILE = 512
B_GROUP = 4
B_VT_ROWS = V_HEAD + 16
LAT_COLS = Q_LORA + KV_LORA + 2 * LANE


def _cparams(semantics):
    return pltpu.CompilerParams(dimension_semantics=semantics, vmem_limit_bytes=VMEM_LIMIT_BYTES)


def _layer_norm(x, g, b):
    mu = jnp.mean(x, axis=-1, keepdims=True)
    xc = x - mu
    var = jnp.mean(xc * xc, axis=-1, keepdims=True)
    return xc * lax.rsqrt(var + LN_EPS) * g + b


def _rms_norm(x, g):
    return x * lax.rsqrt(jnp.mean(x * x, axis=-1, keepdims=True) + RMS_EPS) * g


def _dot(a, b):
    return jnp.dot(a, b, preferred_element_type=F32)


def _dot_nt(a, b):
    return lax.dot_general(a, b, (((1,), (1,)), ((), ())), preferred_element_type=F32)


def _ffn_body(x_ref, wg_ref, wu_ref, wo_ref, g_ref, b_ref, o_ref, xb_ref):
    j = pl.program_id(1)

    @pl.when(j == 0)
    def _():
        xb_ref[...] = x_ref[...].astype(BF16)
        o_ref[...] = jnp.zeros_like(o_ref)

    xb = xb_ref[...]
    gate = _dot(xb, wg_ref[...])
    up = _dot(xb, wu_ref[...])
    act = (gate * jax.nn.sigmoid(gate)) * up
    o_ref[...] += _dot(act.astype(BF16), wo_ref[...])

    @pl.when(j == pl.num_programs(1) - 1)
    def _():
        y = ALPHA * x_ref[...] + 0.5 * o_ref[...]
        o_ref[...] = _layer_norm(y, g_ref[...], b_ref[...])


def _ffn_weights(w_in, w_out):
    pad = D_FF_PAD - D_FF
    wg = jnp.pad(w_in[:, :D_FF].astype(BF16), ((0, 0), (0, pad)))
    wu = jnp.pad(w_in[:, D_FF:].astype(BF16), ((0, 0), (0, pad)))
    wo = jnp.pad(w_out.astype(BF16), ((0, pad), (0, 0)))
    return wg, wu, wo


def _ffn_ln(x, w_in, w_out, ln_g, ln_b):
    m = x.shape[0]
    tm = min(FFN_ROW_TILE, m)
    wg, wu, wo = _ffn_weights(w_in, w_out)
    return pl.pallas_call(
        _ffn_body,
        grid=(m // tm, D_FF_PAD // FF_TILE),
        in_specs=[
            pl.BlockSpec((tm, D_MODEL), lambda i, j: (i, 0), pipeline_mode=pl.Buffered(1)),
            pl.BlockSpec((D_MODEL, FF_TILE), lambda i, j: (0, j)),
            pl.BlockSpec((D_MODEL, FF_TILE), lambda i, j: (0, j)),
            pl.BlockSpec((FF_TILE, D_MODEL), lambda i, j: (j, 0)),
            pl.BlockSpec((1, D_MODEL), lambda i, j: (0, 0)),
            pl.BlockSpec((1, D_MODEL), lambda i, j: (0, 0)),
        ],
        out_specs=pl.BlockSpec((tm, D_MODEL), lambda i, j: (i, 0)),
        out_shape=jax.ShapeDtypeStruct((m, D_MODEL), F32),
        scratch_shapes=[pltpu.VMEM((tm, D_MODEL), BF16)],
        compiler_params=_cparams(("parallel", "arbitrary")),
        name="ffn_ln",
    )(x, wg, wu, wo, ln_g.reshape(1, -1), ln_b.reshape(1, -1))


def _mm_body(*refs, pro, has_bias, act, out_scale):
    it = iter(refs)
    x_ref = next(it)
    pg_ref = next(it) if pro else None
    pb_ref = next(it) if pro == "ln" else None
    w_ref = next(it)
    bias_ref = next(it) if has_bias else None
    o_ref = next(it)
    x = x_ref[...]
    if pro == "ln":
        x = _layer_norm(x.astype(F32), pg_ref[...], pb_ref[...])
    elif pro == "rms":
        x = _rms_norm(x.astype(F32), pg_ref[...])
    y = _dot(x.astype(BF16), w_ref[...])
    if has_bias:
        y = y + bias_ref[...]
    if act == "sigmoid":
        y = jax.nn.sigmoid(y)
    if out_scale is not None:
        y = y * out_scale
    o_ref[...] = y.astype(o_ref.dtype)


def _matmul(x, w, *, out_dtype, tn, x_cols=None, pro=None, pro_g=None, pro_b=None,
            bias=None, act=None, out_scale=None, name="matmul"):
    m = x.shape[0]
    k, n = w.shape
    tm = min(ROW_TILE, m)
    xblk = 0 if x_cols is None else x_cols
    args = [x]
    specs = [pl.BlockSpec((tm, k), lambda i, j: (i, xblk))]
    if pro:
        args.append(pro_g.reshape(1, k))
        specs.append(pl.BlockSpec((1, k), lambda i, j: (0, 0)))
    if pro == "ln":
        args.append(pro_b.reshape(1, k))
        specs.append(pl.BlockSpec((1, k), lambda i, j: (0, 0)))
    args.append(w)
    specs.append(pl.BlockSpec((k, tn), lambda i, j: (0, j)))
    if bias is not None:
        args.append(bias.reshape(1, n))
        specs.append(pl.BlockSpec((1, tn), lambda i, j: (0, j)))
    body = functools.partial(_mm_body, pro=pro, has_bias=bias is not None, act=act, out_scale=out_scale)
    return pl.pallas_call(
        body,
        grid=(m // tm, n // tn),
        in_specs=specs,
        out_specs=pl.BlockSpec((tm, tn), lambda i, j: (i, j)),
        out_shape=jax.ShapeDtypeStruct((m, n), out_dtype),
        compiler_params=_cparams(("parallel", "parallel")),
        name=name,
    )(*args)


def _q_up_body(lat_ref, g_ref, w_ref, cos_ref, sin_ref, qn_ref, qp_ref, *, scale):
    c = _rms_norm(lat_ref[...], g_ref[...]).astype(BF16)
    qn_ref[...] = (_dot(c, w_ref[:, :B_WIDTH]) * scale).astype(BF16)
    pa = _dot(c, w_ref[:, B_WIDTH:2 * B_WIDTH])
    pb = _dot(c, w_ref[:, 2 * B_WIDTH:])
    cos = cos_ref[...] * scale
    sin = sin_ref[...] * scale
    for h in range(B_HEADS):
        sl = slice(h * LANE, (h + 1) * LANE)
        qp_ref[:, sl] = (pa[:, sl] * cos + pb[:, sl] * sin).astype(BF16)


def _kv_up_body(c_ref, g_ref, wk_ref, wvt_ref, ka_ref, kb_ref, cos_ref, sin_ref, kn_ref, vt_ref, kpe_ref):
    c = _rms_norm(c_ref[...], g_ref[...]).astype(BF16)
    n_chunk = 1024
    for n in range(B_WIDTH // n_chunk):
        sl = slice(n * n_chunk, (n + 1) * n_chunk)
        kn_ref[:, sl] = _dot(c, wk_ref[:, sl]).astype(BF16)
    heads_per_dot = 4
    tm = c.shape[0]
    ones_rows = (lax.broadcasted_iota(jnp.int32, (B_VT_ROWS - V_HEAD, tm), 0) == 0).astype(BF16)
    for hc in range(B_HEADS // heads_per_dot):
        rows = slice(hc * heads_per_dot * V_HEAD, (hc + 1) * heads_per_dot * V_HEAD)
        vt = _dot_nt(wvt_ref[rows, :], c).astype(BF16)
        for r in range(heads_per_dot):
            vt_ref[0, hc * heads_per_dot + r, 0, :V_HEAD, :] = vt[r * V_HEAD:(r + 1) * V_HEAD, :]
            vt_ref[0, hc * heads_per_dot + r, 0, V_HEAD:, :] = ones_rows
    kpe_ref[...] = (ka_ref[...] * cos_ref[...] + kb_ref[...] * sin_ref[...]).astype(BF16)


def _rope_tables(seq_len):
    half = QK_ROPE // 2
    inv = 1.0 / (ROPE_THETA ** (jnp.arange(0, QK_ROPE, 2, dtype=F32) / QK_ROPE))
    ang = jnp.arange(seq_len, dtype=F32)[:, None] * inv[None, :]
    zeros = jnp.zeros((seq_len, LANE - 2 * half), F32)
    cos = jnp.concatenate([jnp.cos(ang), jnp.cos(ang), zeros], axis=1)
    sin = jnp.concatenate([jnp.sin(ang), jnp.sin(ang), zeros], axis=1)
    return cos, sin


def _rot_pair(x1, x2, pad_shape):
    zeros = jnp.zeros(pad_shape, x1.dtype)
    return (jnp.concatenate([x1, x2, zeros], axis=-1), jnp.concatenate([-x2, x1, zeros], axis=-1))


def _latent_up(lat, q_a_norm, w_q_b, kv_a_norm, w_kv_b, seq_len):
    m = lat.shape[0]
    tm = min(B_TILE, seq_len)
    n_pos = seq_len // tm
    cos, sin = _rope_tables(seq_len)
    half = QK_ROPE // 2
    wq = w_q_b.reshape(Q_LORA, B_HEADS, QK_NOPE + QK_ROPE)
    pe = wq[:, :, QK_NOPE:]
    pe_a, pe_b = _rot_pair(pe[..., :half], pe[..., half:], (Q_LORA, B_HEADS, LANE - QK_ROPE))
    w_q_all = jnp.concatenate([wq[:, :, :QK_NOPE].reshape(Q_LORA, B_WIDTH),
                               pe_a.reshape(Q_LORA, B_WIDTH), pe_b.reshape(Q_LORA, B_WIDTH)],
                              axis=1).astype(BF16)
    wkv = w_kv_b.reshape(KV_LORA, B_HEADS, QK_NOPE + V_HEAD)
    w_k = wkv[:, :, :QK_NOPE].reshape(KV_LORA, B_WIDTH).astype(BF16)
    w_vt = wkv[:, :, QK_NOPE:].reshape(KV_LORA, B_WIDTH).T.astype(BF16)
    scale = (QK_NOPE + QK_ROPE) ** -0.5 * math.log2(math.e)
    row = lambda i: (i, 0)
    const = lambda i: (0, 0)
    pos = lambda i: (i % n_pos, 0)
    q_nope, q_pe = pl.pallas_call(
        functools.partial(_q_up_body, scale=scale),
        grid=(m // tm,),
        in_specs=[
            pl.BlockSpec((tm, Q_LORA), row),
            pl.BlockSpec((1, Q_LORA), const),
            pl.BlockSpec((Q_LORA, 3 * B_WIDTH), const),
            pl.BlockSpec((tm, LANE), pos),
            pl.BlockSpec((tm, LANE), pos),
        ],
        out_specs=[pl.BlockSpec((tm, B_WIDTH), row), pl.BlockSpec((tm, B_WIDTH), row)],
        out_shape=[jax.ShapeDtypeStruct((m, B_WIDTH), BF16), jax.ShapeDtypeStruct((m, B_WIDTH), BF16)],
        compiler_params=_cparams(("parallel",)),
        name="q_up",
    )(lat, q_a_norm.reshape(1, -1), w_q_all, cos, sin)
    ka_blk = (Q_LORA + KV_LORA) // LANE
    batch = m // seq_len
    k_nope, v_t, k_pe = pl.pallas_call(
        _kv_up_body,
        grid=(m // tm,),
        in_specs=[
            pl.BlockSpec((tm, KV_LORA), lambda i: (i, Q_LORA // KV_LORA)),
            pl.BlockSpec((1, KV_LORA), const),
            pl.BlockSpec((KV_LORA, B_WIDTH), const),
            pl.BlockSpec((B_WIDTH, KV_LORA), const),
            pl.BlockSpec((tm, LANE), lambda i: (i, ka_blk)),
            pl.BlockSpec((tm, LANE), lambda i: (i, ka_blk + 1)),
            pl.BlockSpec((tm, LANE), pos),
            pl.BlockSpec((tm, LANE), pos),
        ],
        out_specs=[pl.BlockSpec((tm, B_WIDTH), row),
                   pl.BlockSpec((1, B_HEADS, 1, B_VT_ROWS, tm), lambda i: (i // n_pos, 0, i % n_pos, 0, 0)),
                   pl.BlockSpec((tm, LANE), row)],
        out_shape=[jax.ShapeDtypeStruct((m, B_WIDTH), BF16),
                   jax.ShapeDtypeStruct((batch, B_HEADS, n_pos, B_VT_ROWS, tm), BF16),
                   jax.ShapeDtypeStruct((m, LANE), BF16)],
        compiler_params=_cparams(("parallel",)),
        name="kv_up",
    )(lat, kv_a_norm.reshape(1, -1), w_k, w_vt, lat, lat, cos, sin)
    return q_nope, q_pe, k_nope, v_t, k_pe


def _attn_a_body(q_ref, k0_ref, k1_ref, k2_ref, v0_ref, v1_ref, v2_ref, toep_ref, o_ref, bias_ref):
    i = pl.program_id(1)

    @pl.when((pl.program_id(0) == 0) & (i == 0))
    def _():
        shift = int(math.log2(CHUNK))
        qc = lax.broadcasted_iota(jnp.int32, (A_Q_TILE, A_WIN), 0) >> shift
        kc = lax.broadcasted_iota(jnp.int32, (A_Q_TILE, A_WIN), 1) >> shift
        band = (kc >= qc) & (kc <= qc + LEFT_CHUNKS)
        for h in range(A_HEADS):
            rows = jnp.broadcast_to(toep_ref[h:h + 1, :], (A_Q_TILE, A_TOEP))
            tile = pltpu.roll(rows, 0, 1, stride=1, stride_axis=0)
            bias_ref[h] = jnp.where(band, tile[:, :A_WIN], NEG_INF)

    col = lax.broadcasted_iota(jnp.int32, (A_Q_TILE, A_WIN), 1)
    in_seq = col >= (A_LEFT // A_Q_TILE - i) * A_Q_TILE
    lane = lax.broadcasted_iota(jnp.int32, (1, LANE), 1)
    low = lane < A_HEAD_DIM
    for hp in range(A_WIDTH // LANE):
        sl = slice(hp * LANE, (hp + 1) * LANE)
        qp = q_ref[0, :, sl]
        kp = jnp.concatenate([k0_ref[0, :, sl], k1_ref[0, :, sl], k2_ref[0, :, sl]], axis=0)
        vp = jnp.concatenate([v0_ref[0, :, sl], v1_ref[0, :, sl], v2_ref[0, :, sl]], axis=0)
        outs = []
        for sub in range(2):
            sel = low if sub == 0 else jnp.logical_not(low)
            qm = jnp.where(sel, qp, jnp.zeros_like(qp))
            s = _dot_nt(qm, kp)
            s = jnp.where(in_seq, s + bias_ref[2 * hp + sub], NEG_INF)
            mx = jnp.max(s, axis=-1, keepdims=True)
            e = jnp.exp(s - mx)
            den = jnp.sum(e, axis=-1, keepdims=True)
            outs.append(_dot(e.astype(BF16), vp) / den)
        o_ref[0, :, sl] = jnp.where(low, outs[0], outs[1]).astype(o_ref.dtype)


def _attn_a_toeplitz_row(rel_bias):
    n_head = rel_bias.shape[0]
    far_left = jnp.broadcast_to(rel_bias[:, -1:], (n_head, A_LEFT - MAX_REL))
    far_right = jnp.broadcast_to(rel_bias[:, :1], (n_head, A_WIN - A_LEFT - MAX_REL))
    wrapped = jnp.broadcast_to(rel_bias[:, -1:], (n_head, A_Q_TILE - 1))
    return jnp.concatenate([far_left, rel_bias[:, ::-1], far_right, wrapped], axis=1).astype(F32)


def _attn_a(qkv, rel_bias):
    b, s, _ = qkv.shape
    nq = s // A_Q_TILE
    back = A_LEFT // A_Q_TILE
    blk = (1, A_Q_TILE, A_WIDTH)

    def kv_spec(col, off):
        return pl.BlockSpec(blk, lambda bi, i: (bi, jnp.maximum(i - back + off, 0), col))

    return pl.pallas_call(
        _attn_a_body,
        grid=(b, nq),
        in_specs=[pl.BlockSpec(blk, lambda bi, i: (bi, i, 0))]
        + [kv_spec(1, off) for off in range(back + 1)]
        + [kv_spec(2, off) for off in range(back + 1)]
        + [pl.BlockSpec((A_HEADS, A_TOEP), lambda bi, i: (0, 0))],
        out_specs=pl.BlockSpec(blk, lambda bi, i: (bi, i, 0)),
        out_shape=jax.ShapeDtypeStruct((b, s, A_WIDTH), BF16),
        scratch_shapes=[pltpu.VMEM((A_HEADS, A_Q_TILE, A_WIN), F32)],
        compiler_params=_cparams(("arbitrary", "arbitrary")),
        name="attn_a",
    )(qkv, qkv, qkv, qkv, qkv, qkv, qkv, _attn_a_toeplitz_row(rel_bias))


def _attn_b_body(qn_ref, qp_ref, kn_ref, kpe_ref, vt_ref, o_ref, m_ref, acc_ref):
    qi = pl.program_id(2)
    t = qn_ref.shape[1]
    slabs = [slice(g * LANE, (g + 1) * LANE) for g in range(B_GROUP)]
    qs = [jnp.concatenate([qn_ref[0, :, sl], qp_ref[0, :, sl]], axis=-1) for sl in slabs]
    shift = int(math.log2(CHUNK))
    key_chunk = lax.broadcasted_iota(jnp.int32, (t, t), 0) >> shift
    qry_chunk = lax.broadcasted_iota(jnp.int32, (t, t), 1) >> shift
    diag_allowed = key_chunk <= qry_chunk

    m_ref[...] = jnp.full_like(m_ref, NEG_INF)
    acc_ref[...] = jnp.zeros_like(acc_ref)

    def step(j, masked):
        rows = pl.ds(pl.multiple_of(j * t, t), t)
        kpe = kpe_ref[0, rows, :]
        scores = [_dot_nt(jnp.concatenate([kn_ref[0, rows, sl], kpe], axis=-1), qs[g])
                  for g, sl in enumerate(slabs)]
        for g in range(B_GROUP):
            st = scores[g]
            if masked:
                st = jnp.where(diag_allowed, st, NEG_INF)
            m_old = m_ref[g]
            m_new = jnp.maximum(m_old, jnp.max(st, axis=0, keepdims=True))
            corr = jnp.exp2(m_old - m_new)
            p = jnp.exp2(st - m_new).astype(BF16)
            m_ref[g] = m_new
            acc_ref[g] = corr * acc_ref[g] + _dot(vt_ref[0, g, j], p)

    def full_step(j, carry):
        step(j, False)
        return carry

    lax.fori_loop(0, qi, full_step, 0)
    step(qi, True)
    for g, sl in enumerate(slabs):
        o_ref[0, :, sl] = (acc_ref[g, :V_HEAD, :] / acc_ref[g, V_HEAD:V_HEAD + 1, :]).T.astype(o_ref.dtype)


def _attn_b(q_nope, q_pe, k_nope, k_pe, v_t):
    b, s, _ = q_nope.shape
    t = v_t.shape[-1]
    width = B_GROUP * LANE
    qblk = (1, t, width)
    return pl.pallas_call(
        _attn_b_body,
        grid=(b, B_HEADS // B_GROUP, s // t),
        in_specs=[
            pl.BlockSpec(qblk, lambda bi, hg, i: (bi, i, hg)),
            pl.BlockSpec(qblk, lambda bi, hg, i: (bi, i, hg)),
            pl.BlockSpec((1, s, width), lambda bi, hg, i: (bi, 0, hg)),
            pl.BlockSpec((1, s, LANE), lambda bi, hg, i: (bi, 0, 0)),
            pl.BlockSpec((1, B_GROUP, s // t, B_VT_ROWS, t), lambda bi, hg, i: (bi, hg, 0, 0, 0)),
        ],
        out_specs=pl.BlockSpec(qblk, lambda bi, hg, i: (bi, i, hg)),
        out_shape=jax.ShapeDtypeStruct((b, s, B_WIDTH), BF16),
        scratch_shapes=[pltpu.VMEM((B_GROUP, 1, t), F32), pltpu.VMEM((B_GROUP, B_VT_ROWS, t), F32)],
        compiler_params=_cparams(("parallel", "parallel", "arbitrary")),
        name="attn_b",
    )(q_nope, q_pe, k_nope, k_pe, v_t)


def _gate_mix_body(oa_ref, ob_ref, ga_ref, gb_ref, wa_ref, wb_ref, z_ref):
    ya = _dot(oa_ref[...], wa_ref[...])
    yb = _dot(ob_ref[...], wb_ref[...])
    z_ref[...] = (ga_ref[...].astype(F32) * ya + gb_ref[...].astype(F32) * yb).astype(z_ref.dtype)


def _gate_mix(o_a, o_b, gates, w_o_a, w_o_b):
    m = o_a.shape[0]
    tm = min(ROW_TILE, m)
    tn = 1024
    nj = D_MODEL // tn
    return pl.pallas_call(
        _gate_mix_body,
        grid=(m // tm, nj),
        in_specs=[
            pl.BlockSpec((tm, A_WIDTH), lambda i, j: (i, 0)),
            pl.BlockSpec((tm, B_WIDTH), lambda i, j: (i, 0)),
            pl.BlockSpec((tm, tn), lambda i, j: (i, j)),
            pl.BlockSpec((tm, tn), lambda i, j: (i, nj + j)),
            pl.BlockSpec((A_WIDTH, tn), lambda i, j: (0, j)),
            pl.BlockSpec((B_WIDTH, tn), lambda i, j: (0, j)),
        ],
        out_specs=pl.BlockSpec((tm, tn), lambda i, j: (i, j)),
        out_shape=jax.ShapeDtypeStruct((m, D_MODEL), BF16),
        compiler_params=_cparams(("parallel", "parallel")),
        name="gate_mix",
    )(o_a, o_b, gates, gates, w_o_a.astype(BF16), w_o_b.astype(BF16))


def _proj_res_ln_body(x_ref, w_ref, res_ref, g_ref, b_ref, o_ref):
    y = ALPHA * res_ref[...] + _dot(x_ref[...], w_ref[...])
    o_ref[...] = _layer_norm(y, g_ref[...], b_ref[...])


def _proj_res_ln(x, w, res, ln_g, ln_b):
    m, k = x.shape
    tm = min(ROW_TILE, m)
    row = lambda i: (i, 0)
    const = lambda i: (0, 0)
    return pl.pallas_call(
        _proj_res_ln_body,
        grid=(m // tm,),
        in_specs=[
            pl.BlockSpec((tm, k), row),
            pl.BlockSpec((k, D_MODEL), const),
            pl.BlockSpec((tm, D_MODEL), row),
            pl.BlockSpec((1, D_MODEL), const),
            pl.BlockSpec((1, D_MODEL), const),
        ],
        out_specs=pl.BlockSpec((tm, D_MODEL), row),
        out_shape=jax.ShapeDtypeStruct((m, D_MODEL), F32),
        compiler_params=_cparams(("parallel",)),
        name="proj_res_ln",
    )(x, w.astype(BF16), res, ln_g.reshape(1, -1), ln_b.reshape(1, -1))


def _xattn_body(q_ref, kv_ref, w_ref, res_ref, g_ref, b_ref, o_ref, att_ref):
    for h in range(X_HEADS):
        sl = slice(h * X_HEAD_DIM, (h + 1) * X_HEAD_DIM)
        vs = slice(D_MODEL + h * X_HEAD_DIM, D_MODEL + (h + 1) * X_HEAD_DIM)
        s = _dot_nt(q_ref[0, :, sl], kv_ref[0, :, sl])
        mx = jnp.max(s, axis=-1, keepdims=True)
        e = jnp.exp(s - mx)
        den = jnp.sum(e, axis=-1, keepdims=True)
        att_ref[:, sl] = (_dot(e.astype(BF16), kv_ref[0, :, vs]) / den).astype(BF16)
    y = ALPHA * res_ref[0] + _dot(att_ref[...], w_ref[...])
    o_ref[0] = _layer_norm(y, g_ref[...], b_ref[...])


def _xattn(q, kv, w_xo, res, ln_g, ln_b):
    b, s, _ = q.shape
    tm = min(ROW_TILE, s)
    m_len = kv.shape[1]
    row = lambda bi, i: (bi, i, 0)
    return pl.pallas_call(
        _xattn_body,
        grid=(b, s // tm),
        in_specs=[
            pl.BlockSpec((1, tm, D_MODEL), row),
            pl.BlockSpec((1, m_len, 2 * D_MODEL), lambda bi, i: (bi, 0, 0)),
            pl.BlockSpec((D_MODEL, D_MODEL), lambda bi, i: (0, 0)),
            pl.BlockSpec((1, tm, D_MODEL), row),
            pl.BlockSpec((1, D_MODEL), lambda bi, i: (0, 0)),
            pl.BlockSpec((1, D_MODEL), lambda bi, i: (0, 0)),
        ],
        out_specs=pl.BlockSpec((1, tm, D_MODEL), row),
        out_shape=jax.ShapeDtypeStruct((b, s, D_MODEL), F32),
        scratch_shapes=[pltpu.VMEM((tm, D_MODEL), BF16)],
        compiler_params=_cparams(("parallel", "parallel")),
        name="xattn",
    )(q, kv, w_xo.astype(BF16), res, ln_g.reshape(1, -1), ln_b.reshape(1, -1))


def _in_proj_weights(w_in):
    qkv_end = 3 * A_WIDTH
    lat_end = qkv_end + Q_LORA + KV_LORA
    half = QK_ROPE // 2
    col_scale = jnp.concatenate([jnp.full((A_WIDTH,), A_HEAD_DIM ** -0.5, F32), jnp.ones((2 * A_WIDTH,), F32)])
    w_qkv = (w_in[:, :qkv_end] * col_scale).astype(BF16)
    x1 = w_in[:, lat_end:lat_end + half]
    x2 = w_in[:, lat_end + half:lat_end + QK_ROPE]
    kpe_a, kpe_b = _rot_pair(x1, x2, (D_MODEL, LANE - QK_ROPE))
    w_lat = jnp.concatenate([w_in[:, qkv_end:lat_end], kpe_a, kpe_b], axis=1).astype(BF16)
    w_gate = w_in[:, lat_end + QK_ROPE:].astype(BF16)
    return w_qkv, w_lat, w_gate


def kernel(x, mem, ffn1_w_in, ffn1_w_out, ln_ffn1_g, ln_ffn1_b, w_in, gate_bias, rel_bias, q_a_norm, w_q_b,
           kv_a_norm, w_kv_b, w_o_a, w_o_b, w_out, ln_mix_g, ln_mix_b, mem_ln_g, mem_ln_b, w_xq, w_xkv, w_xo,
           ln_x_g, ln_x_b, ffn2_w_in, ffn2_w_out, ln_ffn2_g, ln_ffn2_b):
    b, s, d = x.shape
    m = b * s
    h = x.reshape(m, d)
    for l in range(DEPTH):
        h = _ffn_ln(h, ffn1_w_in[l], ffn1_w_out[l], ln_ffn1_g[l], ln_ffn1_b[l])

        w_qkv, w_lat, w_gate = _in_proj_weights(w_in[l])
        qkv = _matmul(h, w_qkv, out_dtype=BF16, tn=1024, name="in_proj_qkv")
        gates = _matmul(h, w_gate, out_dtype=BF16, tn=1024, bias=gate_bias[l], act="sigmoid", name="in_proj_gates")
        lat = _matmul(h, w_lat, out_dtype=F32, tn=LAT_COLS, name="in_proj_latent")
        o_a = _attn_a(qkv.reshape(b, s, 3 * A_WIDTH), rel_bias[l])
        q_nope, q_pe, k_nope, v_t, k_pe = _latent_up(lat, q_a_norm[l], w_q_b[l], kv_a_norm[l], w_kv_b[l], s)
        o_b = _attn_b(q_nope.reshape(b, s, B_WIDTH), q_pe.reshape(b, s, B_WIDTH),
                      k_nope.reshape(b, s, B_WIDTH), k_pe.reshape(b, s, LANE), v_t)
        z = _gate_mix(o_a.reshape(m, A_WIDTH), o_b.reshape(m, B_WIDTH), gates, w_o_a[l], w_o_b[l])
        h = _proj_res_ln(z, w_out[l], h, ln_mix_g[l], ln_mix_b[l])

        m_len = mem.shape[1]
        kv_x = _matmul(mem.reshape(b * m_len, d), w_xkv[l].astype(BF16), out_dtype=BF16, tn=1024,
                       pro="ln", pro_g=mem_ln_g[l], pro_b=mem_ln_b[l], name="mem_kv")
        q_x = _matmul(h, w_xq[l].astype(BF16), out_dtype=BF16, tn=1024, out_scale=X_HEAD_DIM ** -0.5, name="xattn_q")
        h = _xattn(q_x.reshape(b, s, d), kv_x.reshape(b, m_len, 2 * d), w_xo[l], h.reshape(b, s, d),
                   ln_x_g[l], ln_x_b[l]).reshape(m, d)

        h = _ffn_ln(h, ffn2_w_in[l], ffn2_w_out[l], ln_ffn2_g[l], ln_ffn2_b[l])
    return h.reshape(b, s, d)
```

```python
import functools
import math

import jax
import jax.numpy as jnp
from jax import lax
from jax.experimental import pallas as pl
from jax.experimental.pallas import tpu as pltpu

BF16 = jnp.bfloat16
F32 = jnp.float32

D_MODEL = 2048
CHUNK = 64
LEFT_CHUNKS = 8
MEM_LEN = 256
A_HEADS = 16
A_HEAD_DIM = 64
A_WIDTH = A_HEADS * A_HEAD_DIM
MAX_REL = 128
B_HEADS = 16
QK_NOPE = 128
QK_ROPE = 64
V_HEAD = 128
Q_LORA = 512
KV_LORA = 512
ROPE_THETA = 10000.0
B_WIDTH = B_HEADS * V_HEAD
X_HEADS = 4
X_HEAD_DIM = D_MODEL // X_HEADS
D_FF = 5504
DEPTH = 1
ALPHA = (2.0 * DEPTH) ** 0.25
LN_EPS = 1e-5
RMS_EPS = 1e-6
NEG_INF = -1e30

LANE = 128
VMEM_LIMIT_BYTES = 56 * 1024 * 1024

FF_TILE = 512
D_FF_PAD = ((D_FF + FF_TILE - 1) // FF_TILE) * FF_TILE
ROW_TILE = 512
FFN_ROW_TILE = 512
A_Q_TILE = 256
A_LEFT = LEFT_CHUNKS * CHUNK
A_WIN = A_LEFT + A_Q_TILE
A_TOEP = A_Q_TILE + A_WIN
B_TILE = 512
B_GROUP = 4
B_VT_ROWS = V_HEAD + 16
LAT_COLS = Q_LORA + KV_LORA + 2 * LANE


def _cparams(semantics):
    return pltpu.CompilerParams(dimension_semantics=semantics, vmem_limit_bytes=VMEM_LIMIT_BYTES)


def _layer_norm(x, g, b):
    mu = jnp.mean(x, axis=-1, keepdims=True)
    xc = x - mu
    var = jnp.mean(xc * xc, axis=-1, keepdims=True)
    return xc * lax.rsqrt(var + LN_EPS) * g + b


def _rms_norm(x, g):
    return x * lax.rsqrt(jnp.mean(x * x, axis=-1, keepdims=True) + RMS_EPS) * g


def _dot(a, b):
    return jnp.dot(a, b, preferred_element_type=F32)


def _dot_nt(a, b):
    return lax.dot_general(a, b, (((1,), (1,)), ((), ())), preferred_element_type=F32)


def _ffn_body(x_ref, wg_ref, wu_ref, wo_ref, g_ref, b_ref, o_ref, xb_ref):
    j = pl.program_id(1)

    @pl.when(j == 0)
    def _():
        xb_ref[...] = x_ref[...].astype(BF16)
        o_ref[...] = jnp.zeros_like(o_ref)

    xb = xb_ref[...]
    gate = _dot(xb, wg_ref[...])
    up = _dot(xb, wu_ref[...])
    act = (gate * jax.nn.sigmoid(gate)) * up
    o_ref[...] += _dot(act.astype(BF16), wo_ref[...])

    @pl.when(j == pl.num_programs(1) - 1)
    def _():
        y = ALPHA * x_ref[...] + 0.5 * o_ref[...]
        o_ref[...] = _layer_norm(y, g_ref[...], b_ref[...])


def _ffn_weights(w_in, w_out):
    pad = D_FF_PAD - D_FF
    wg = jnp.pad(w_in[:, :D_FF].astype(BF16), ((0, 0), (0, pad)))
    wu = jnp.pad(w_in[:, D_FF:].astype(BF16), ((0, 0), (0, pad)))
    wo = jnp.pad(w_out.astype(BF16), ((0, pad), (0, 0)))
    return wg, wu, wo


def _ffn_ln(x, w_in, w_out, ln_g, ln_b):
    m = x.shape[0]
    tm = min(FFN_ROW_TILE, m)
    wg, wu, wo = _ffn_weights(w_in, w_out)
    return pl.pallas_call(
        _ffn_body,
        grid=(m // tm, D_FF_PAD // FF_TILE),
        in_specs=[
            pl.BlockSpec((tm, D_MODEL), lambda i, j: (i, 0)),
            pl.BlockSpec((D_MODEL, FF_TILE), lambda i, j: (0, j)),
            pl.BlockSpec((D_MODEL, FF_TILE), lambda i, j: (0, j)),
            pl.BlockSpec((FF_TILE, D_MODEL), lambda i, j: (j, 0)),
            pl.BlockSpec((1, D_MODEL), lambda i, j: (0, 0)),
            pl.BlockSpec((1, D_MODEL), lambda i, j: (0, 0)),
        ],
        out_specs=pl.BlockSpec((tm, D_MODEL), lambda i, j: (i, 0)),
        out_shape=jax.ShapeDtypeStruct((m, D_MODEL), F32),
        scratch_shapes=[pltpu.VMEM((tm, D_MODEL), BF16)],
        compiler_params=_cparams(("parallel", "arbitrary")),
        name="ffn_ln",
    )(x, wg, wu, wo, ln_g.reshape(1, -1), ln_b.reshape(1, -1))


def _mm_body(*refs, pro, has_bias, act, out_scale):
    it = iter(refs)
    x_ref = next(it)
    pg_ref = next(it) if pro else None
    pb_ref = next(it) if pro == "ln" else None
    w_ref = next(it)
    bias_ref = next(it) if has_bias else None
    o_ref = next(it)
    x = x_ref[...]
    if pro == "ln":
        x = _layer_norm(x.astype(F32), pg_ref[...], pb_ref[...])
    elif pro == "rms":
        x = _rms_norm(x.astype(F32), pg_ref[...])
    y = _dot(x.astype(BF16), w_ref[...])
    if has_bias:
        y = y + bias_ref[...]
    if act == "sigmoid":
        y = jax.nn.sigmoid(y)
    if out_scale is not None:
        y = y * out_scale
    o_ref[...] = y.astype(o_ref.dtype)


def _matmul(x, w, *, out_dtype, tn, x_cols=None, pro=None, pro_g=None, pro_b=None,
            bias=None, act=None, out_scale=None, name="matmul"):
    m = x.shape[0]
    k, n = w.shape
    tm = min(ROW_TILE, m)
    xblk = 0 if x_cols is None else x_cols
    args = [x]
    specs = [pl.BlockSpec((tm, k), lambda i, j: (i, xblk))]
    if pro:
        args.append(pro_g.reshape(1, k))
        specs.append(pl.BlockSpec((1, k), lambda i, j: (0, 0)))
    if pro == "ln":
        args.append(pro_b.reshape(1, k))
        specs.append(pl.BlockSpec((1, k), lambda i, j: (0, 0)))
    args.append(w)
    specs.append(pl.BlockSpec((k, tn), lambda i, j: (0, j)))
    if bias is not None:
        args.append(bias.reshape(1, n))
        specs.append(pl.BlockSpec((1, tn), lambda i, j: (0, j)))
    body = functools.partial(_mm_body, pro=pro, has_bias=bias is not None, act=act, out_scale=out_scale)
    return pl.pallas_call(
        body,
        grid=(m // tm, n // tn),
        in_specs=specs,
        out_specs=pl.BlockSpec((tm, tn), lambda i, j: (i, j)),
        out_shape=jax.ShapeDtypeStruct((m, n), out_dtype),
        compiler_params=_cparams(("parallel", "parallel")),
        name=name,
    )(*args)


def _q_up_body(lat_ref, g_ref, w_ref, cos_ref, sin_ref, qn_ref, qp_ref, *, scale):
    c = _rms_norm(lat_ref[...], g_ref[...]).astype(BF16)
    qn_ref[...] = (_dot(c, w_ref[:, :B_WIDTH]) * scale).astype(BF16)
    pa = _dot(c, w_ref[:, B_WIDTH:2 * B_WIDTH])
    pb = _dot(c, w_ref[:, 2 * B_WIDTH:])
    cos = cos_ref[...] * scale
    sin = sin_ref[...] * scale
    for h in range(B_HEADS):
        sl = slice(h * LANE, (h + 1) * LANE)
        qp_ref[:, sl] = (pa[:, sl] * cos + pb[:, sl] * sin).astype(BF16)


def _kv_up_body(c_ref, g_ref, wk_ref, wvt_ref, ka_ref, kb_ref, cos_ref, sin_ref, kn_ref, vt_ref, kpe_ref):
    c = _rms_norm(c_ref[...], g_ref[...]).astype(BF16)
    n_chunk = 1024
    for n in range(B_WIDTH // n_chunk):
        sl = slice(n * n_chunk, (n + 1) * n_chunk)
        kn_ref[:, sl] = _dot(c, wk_ref[:, sl]).astype(BF16)
    heads_per_dot = 4
    tm = c.shape[0]
    ones_rows = (lax.broadcasted_iota(jnp.int32, (B_VT_ROWS - V_HEAD, tm), 0) == 0).astype(BF16)
    for hc in range(B_HEADS // heads_per_dot):
        rows = slice(hc * heads_per_dot * V_HEAD, (hc + 1) * heads_per_dot * V_HEAD)
        vt = _dot_nt(wvt_ref[rows, :], c).astype(BF16)
        for r in range(heads_per_dot):
            vt_ref[0, hc * heads_per_dot + r, 0, :V_HEAD, :] = vt[r * V_HEAD:(r + 1) * V_HEAD, :]
            vt_ref[0, hc * heads_per_dot + r, 0, V_HEAD:, :] = ones_rows
    kpe_ref[...] = (ka_ref[...] * cos_ref[...] + kb_ref[...] * sin_ref[...]).astype(BF16)


def _rope_tables(seq_len):
    half = QK_ROPE // 2
    inv = 1.0 / (ROPE_THETA ** (jnp.arange(0, QK_ROPE, 2, dtype=F32) / QK_ROPE))
    ang = jnp.arange(seq_len, dtype=F32)[:, None] * inv[None, :]
    zeros = jnp.zeros((seq_len, LANE - 2 * half), F32)
    cos = jnp.concatenate([jnp.cos(ang), jnp.cos(ang), zeros], axis=1)
    sin = jnp.concatenate([jnp.sin(ang), jnp.sin(ang), zeros], axis=1)
    return cos, sin


def _rot_pair(x1, x2, pad_shape):
    zeros = jnp.zeros(pad_shape, x1.dtype)
    return (jnp.concatenate([x1, x2, zeros], axis=-1), jnp.concatenate([-x2, x1, zeros], axis=-1))


def _latent_up(lat, q_a_norm, w_q_b, kv_a_norm, w_kv_b, seq_len):
    m = lat.shape[0]
    tm = min(B_TILE, seq_len)
    n_pos = seq_len // tm
    cos, sin = _rope_tables(seq_len)
    half = QK_ROPE // 2
    wq = w_q_b.reshape(Q_LORA, B_HEADS, QK_NOPE + QK_ROPE)
    pe = wq[:, :, QK_NOPE:]
    pe_a, pe_b = _rot_pair(pe[..., :half], pe[..., half:], (Q_LORA, B_HEADS, LANE - QK_ROPE))
    w_q_all = jnp.concatenate([wq[:, :, :QK_NOPE].reshape(Q_LORA, B_WIDTH),
                               pe_a.reshape(Q_LORA, B_WIDTH), pe_b.reshape(Q_LORA, B_WIDTH)],
                              axis=1).astype(BF16)
    wkv = w_kv_b.reshape(KV_LORA, B_HEADS, QK_NOPE + V_HEAD)
    w_k = wkv[:, :, :QK_NOPE].reshape(KV_LORA, B_WIDTH).astype(BF16)
    w_vt = wkv[:, :, QK_NOPE:].reshape(KV_LORA, B_WIDTH).T.astype(BF16)
    scale = (QK_NOPE + QK_ROPE) ** -0.5 * math.log2(math.e)
    row = lambda i: (i, 0)
    const = lambda i: (0, 0)
    pos = lambda i: (i % n_pos, 0)
    q_nope, q_pe = pl.pallas_call(
        functools.partial(_q_up_body, scale=scale),
        grid=(m // tm,),
        in_specs=[
            pl.BlockSpec((tm, Q_LORA), row),
            pl.BlockSpec((1, Q_LORA), const),
            pl.BlockSpec((Q_LORA, 3 * B_WIDTH), const),
            pl.BlockSpec((tm, LANE), pos),
            pl.BlockSpec((tm, LANE), pos),
        ],
        out_specs=[pl.BlockSpec((tm, B_WIDTH), row), pl.BlockSpec((tm, B_WIDTH), row)],
        out_shape=[jax.ShapeDtypeStruct((m, B_WIDTH), BF16), jax.ShapeDtypeStruct((m, B_WIDTH), BF16)],
        compiler_params=_cparams(("parallel",)),
        name="q_up",
    )(lat, q_a_norm.reshape(1, -1), w_q_all, cos, sin)
    ka_blk = (Q_LORA + KV_LORA) // LANE
    batch = m // seq_len
    k_nope, v_t, k_pe = pl.pallas_call(
        _kv_up_body,
        grid=(m // tm,),
        in_specs=[
            pl.BlockSpec((tm, KV_LORA), lambda i: (i, Q_LORA // KV_LORA)),
            pl.BlockSpec((1, KV_LORA), const),
            pl.BlockSpec((KV_LORA, B_WIDTH), const),
            pl.BlockSpec((B_WIDTH, KV_LORA), const),
            pl.BlockSpec((tm, LANE), lambda i: (i, ka_blk)),
            pl.BlockSpec((tm, LANE), lambda i: (i, ka_blk + 1)),
            pl.BlockSpec((tm, LANE), pos),
            pl.BlockSpec((tm, LANE), pos),
        ],
        out_specs=[pl.BlockSpec((tm, B_WIDTH), row),
                   pl.BlockSpec((1, B_HEADS, 1, B_VT_ROWS, tm), lambda i: (i // n_pos, 0, i % n_pos, 0, 0)),
                   pl.BlockSpec((tm, LANE), row)],
        out_shape=[jax.ShapeDtypeStruct((m, B_WIDTH), BF16),
                   jax.ShapeDtypeStruct((batch, B_HEADS, n_pos, B_VT_ROWS, tm), BF16),
                   jax.ShapeDtypeStruct((m, LANE), BF16)],
        compiler_params=_cparams(("parallel",)),
        name="kv_up",
    )(lat, kv_a_norm.reshape(1, -1), w_k, w_vt, lat, lat, cos, sin)
    return q_nope, q_pe, k_nope, v_t, k_pe


def _attn_a_body(q_ref, k0_ref, k1_ref, k2_ref, v0_ref, v1_ref, v2_ref, toep_ref, o_ref, bias_ref):
    i = pl.program_id(1)

    @pl.when((pl.program_id(0) == 0) & (i == 0))
    def _():
        shift = int(math.log2(CHUNK))
        qc = lax.broadcasted_iota(jnp.int32, (A_Q_TILE, A_WIN), 0) >> shift
        kc = lax.broadcasted_iota(jnp.int32, (A_Q_TILE, A_WIN), 1) >> shift
        band = (kc >= qc) & (kc <= qc + LEFT_CHUNKS)
        for h in range(A_HEADS):
            rows = jnp.broadcast_to(toep_ref[h:h + 1, :], (A_Q_TILE, A_TOEP))
            tile = pltpu.roll(rows, 0, 1, stride=1, stride_axis=0)
            bias_ref[h] = jnp.where(band, tile[:, :A_WIN], NEG_INF)

    col = lax.broadcasted_iota(jnp.int32, (A_Q_TILE, A_WIN), 1)
    in_seq = col >= (A_LEFT // A_Q_TILE - i) * A_Q_TILE
    lane = lax.broadcasted_iota(jnp.int32, (1, LANE), 1)
    low = lane < A_HEAD_DIM
    for hp in range(A_WIDTH // LANE):
        sl = slice(hp * LANE, (hp + 1) * LANE)
        qp = q_ref[0, :, sl]
        kp = jnp.concatenate([k0_ref[0, :, sl], k1_ref[0, :, sl], k2_ref[0, :, sl]], axis=0)
        vp = jnp.concatenate([v0_ref[0, :, sl], v1_ref[0, :, sl], v2_ref[0, :, sl]], axis=0)
        outs = []
        for sub in range(2):
            sel = low if sub == 0 else jnp.logical_not(low)
            qm = jnp.where(sel, qp, jnp.zeros_like(qp))
            s = _dot_nt(qm, kp)
            s = jnp.where(in_seq, s + bias_ref[2 * hp + sub], NEG_INF)
            mx = jnp.max(s, axis=-1, keepdims=True)
            e = jnp.exp(s - mx)
            den = jnp.sum(e, axis=-1, keepdims=True)
            outs.append(_dot(e.astype(BF16), vp) / den)
        o_ref[0, :, sl] = jnp.where(low, outs[0], outs[1]).astype(o_ref.dtype)


def _attn_a_toeplitz_row(rel_bias):
    n_head = rel_bias.shape[0]
    far_left = jnp.broadcast_to(rel_bias[:, -1:], (n_head, A_LEFT - MAX_REL))
    far_right = jnp.broadcast_to(rel_bias[:, :1], (n_head, A_WIN - A_LEFT - MAX_REL))
    wrapped = jnp.broadcast_to(rel_bias[:, -1:], (n_head, A_Q_TILE - 1))
    return jnp.concatenate([far_left, rel_bias[:, ::-1], far_right, wrapped], axis=1).astype(F32)


def _attn_a(qkv, rel_bias):
    b, s, _ = qkv.shape
    nq = s // A_Q_TILE
    back = A_LEFT // A_Q_TILE
    blk = (1, A_Q_TILE, A_WIDTH)

    def kv_spec(col, off):
        return pl.BlockSpec(blk, lambda bi, i: (bi, jnp.maximum(i - back + off, 0), col))

    return pl.pallas_call(
        _attn_a_body,
        grid=(b, nq),
        in_specs=[pl.BlockSpec(blk, lambda bi, i: (bi, i, 0))]
        + [kv_spec(1, off) for off in range(back + 1)]
        + [kv_spec(2, off) for off in range(back + 1)]
        + [pl.BlockSpec((A_HEADS, A_TOEP), lambda bi, i: (0, 0))],
        out_specs=pl.BlockSpec(blk, lambda bi, i: (bi, i, 0)),
        out_shape=jax.ShapeDtypeStruct((b, s, A_WIDTH), BF16),
        scratch_shapes=[pltpu.VMEM((A_HEADS, A_Q_TILE, A_WIN), F32)],
        compiler_params=_cparams(("arbitrary", "arbitrary")),
        name="attn_a",
    )(qkv, qkv, qkv, qkv, qkv, qkv, qkv, _attn_a_toeplitz_row(rel_bias))


def _attn_b_body(qn_ref, qp_ref, kn_ref, kpe_ref, vt_ref, o_ref, m_ref, acc_ref, s_ref):
    qi = pl.program_id(2)
    t = qn_ref.shape[1]
    slabs = [slice(g * LANE, (g + 1) * LANE) for g in range(B_GROUP)]
    qs = [jnp.concatenate([qn_ref[0, :, sl], qp_ref[0, :, sl]], axis=-1) for sl in slabs]
    shift = int(math.log2(CHUNK))
    key_chunk = lax.broadcasted_iota(jnp.int32, (t, t), 0) >> shift
    qry_chunk = lax.broadcasted_iota(jnp.int32, (t, t), 1) >> shift
    diag_allowed = key_chunk <= qry_chunk

    m_ref[...] = jnp.full_like(m_ref, NEG_INF)
    acc_ref[...] = jnp.zeros_like(acc_ref)

    def scores_into(j, slot):
        rows = pl.ds(pl.multiple_of(j * t, t), t)
        kpe = kpe_ref[0, rows, :]
        for g, sl in enumerate(slabs):
            s_ref[slot, g] = _dot_nt(jnp.concatenate([kn_ref[0, rows, sl], kpe], axis=-1), qs[g])

    def consume(j, slot, masked):
        for g in range(B_GROUP):
            st = s_ref[slot, g]
            if masked:
                st = jnp.where(diag_allowed, st, NEG_INF)
            m_old = m_ref[g]
            m_new = jnp.maximum(m_old, jnp.max(st, axis=0, keepdims=True))
            corr = jnp.exp2(m_old - m_new)
            p = jnp.exp2(st - m_new).astype(BF16)
            m_ref[g] = m_new
            acc_ref[g] = corr * acc_ref[g] + _dot(vt_ref[0, g, j], p)

    scores_into(0, 0)

    def pair_step(p, carry):
        j = 2 * p
        scores_into(j + 1, 1)
        consume(j, 0, False)
        scores_into(j + 2, 0)
        consume(j + 1, 1, False)
        return carry

    lax.fori_loop(0, qi // 2, pair_step, 0)

    @pl.when(qi % 2 == 0)
    def _():
        consume(qi, 0, True)

    @pl.when(qi % 2 == 1)
    def _():
        scores_into(qi, 1)
        consume(qi - 1, 0, False)
        consume(qi, 1, True)
    for g, sl in enumerate(slabs):
        o_ref[0, :, sl] = (acc_ref[g, :V_HEAD, :] / acc_ref[g, V_HEAD:V_HEAD + 1, :]).T.astype(o_ref.dtype)


def _attn_b(q_nope, q_pe, k_nope, k_pe, v_t):
    b, s, _ = q_nope.shape
    t = v_t.shape[-1]
    width = B_GROUP * LANE
    qblk = (1, t, width)
    return pl.pallas_call(
        _attn_b_body,
        grid=(b, B_HEADS // B_GROUP, s // t),
        in_specs=[
            pl.BlockSpec(qblk, lambda bi, hg, i: (bi, i, hg)),
            pl.BlockSpec(qblk, lambda bi, hg, i: (bi, i, hg)),
            pl.BlockSpec((1, s, width), lambda bi, hg, i: (bi, 0, hg)),
            pl.BlockSpec((1, s, LANE), lambda bi, hg, i: (bi, 0, 0)),
            pl.BlockSpec((1, B_GROUP, s // t, B_VT_ROWS, t), lambda bi, hg, i: (bi, hg, 0, 0, 0)),
        ],
        out_specs=pl.BlockSpec(qblk, lambda bi, hg, i: (bi, i, hg)),
        out_shape=jax.ShapeDtypeStruct((b, s, B_WIDTH), BF16),
        scratch_shapes=[pltpu.VMEM((B_GROUP, 1, t), F32), pltpu.VMEM((B_GROUP, B_VT_ROWS, t), F32),
                        pltpu.VMEM((2, B_GROUP, t, t), F32)],
        compiler_params=_cparams(("parallel", "parallel", "arbitrary")),
        name="attn_b",
    )(q_nope, q_pe, k_nope, k_pe, v_t)


def _gate_mix_body(oa_ref, ob_ref, ga_ref, gb_ref, wa_ref, wb_ref, z_ref):
    ya = _dot(oa_ref[...], wa_ref[...])
    yb = _dot(ob_ref[...], wb_ref[...])
    z_ref[...] = (ga_ref[...].astype(F32) * ya + gb_ref[...].astype(F32) * yb).astype(z_ref.dtype)


def _gate_mix(o_a, o_b, gates, w_o_a, w_o_b):
    m = o_a.shape[0]
    tm = min(ROW_TILE, m)
    tn = 1024
    nj = D_MODEL // tn
    return pl.pallas_call(
        _gate_mix_body,
        grid=(m // tm, nj),
        in_specs=[
            pl.BlockSpec((tm, A_WIDTH), lambda i, j: (i, 0)),
            pl.BlockSpec((tm, B_WIDTH), lambda i, j: (i, 0)),
            pl.BlockSpec((tm, tn), lambda i, j: (i, j)),
            pl.BlockSpec((tm, tn), lambda i, j: (i, nj + j)),
            pl.BlockSpec((A_WIDTH, tn), lambda i, j: (0, j)),
            pl.BlockSpec((B_WIDTH, tn), lambda i, j: (0, j)),
        ],
        out_specs=pl.BlockSpec((tm, tn), lambda i, j: (i, j)),
        out_shape=jax.ShapeDtypeStruct((m, D_MODEL), BF16),
        compiler_params=_cparams(("parallel", "parallel")),
        name="gate_mix",
    )(o_a, o_b, gates, gates, w_o_a.astype(BF16), w_o_b.astype(BF16))


def _proj_res_ln_body(x_ref, w_ref, res_ref, g_ref, b_ref, o_ref):
    y = ALPHA * res_ref[...] + _dot(x_ref[...], w_ref[...])
    o_ref[...] = _layer_norm(y, g_ref[...], b_ref[...])


def _proj_res_ln(x, w, res, ln_g, ln_b):
    m, k = x.shape
    tm = min(ROW_TILE, m)
    row = lambda i: (i, 0)
    const = lambda i: (0, 0)
    return pl.pallas_call(
        _proj_res_ln_body,
        grid=(m // tm,),
        in_specs=[
            pl.BlockSpec((tm, k), row),
            pl.BlockSpec((k, D_MODEL), const),
            pl.BlockSpec((tm, D_MODEL), row),
            pl.BlockSpec((1, D_MODEL), const),
            pl.BlockSpec((1, D_MODEL), const),
        ],
        out_specs=pl.BlockSpec((tm, D_MODEL), row),
        out_shape=jax.ShapeDtypeStruct((m, D_MODEL), F32),
        compiler_params=_cparams(("parallel",)),
        name="proj_res_ln",
    )(x, w.astype(BF16), res, ln_g.reshape(1, -1), ln_b.reshape(1, -1))


def _xattn_body(q_ref, kv_ref, w_ref, res_ref, g_ref, b_ref, o_ref, att_ref):
    for h in range(X_HEADS):
        sl = slice(h * X_HEAD_DIM, (h + 1) * X_HEAD_DIM)
        vs = slice(D_MODEL + h * X_HEAD_DIM, D_MODEL + (h + 1) * X_HEAD_DIM)
        s = _dot_nt(q_ref[0, :, sl], kv_ref[0, :, sl])
        mx = jnp.max(s, axis=-1, keepdims=True)
        e = jnp.exp(s - mx)
        den = jnp.sum(e, axis=-1, keepdims=True)
        att_ref[:, sl] = (_dot(e.astype(BF16), kv_ref[0, :, vs]) / den).astype(BF16)
    y = ALPHA * res_ref[0] + _dot(att_ref[...], w_ref[...])
    o_ref[0] = _layer_norm(y, g_ref[...], b_ref[...])


def _xattn(q, kv, w_xo, res, ln_g, ln_b):
    b, s, _ = q.shape
    tm = min(ROW_TILE, s)
    m_len = kv.shape[1]
    row = lambda bi, i: (bi, i, 0)
    return pl.pallas_call(
        _xattn_body,
        grid=(b, s // tm),
        in_specs=[
            pl.BlockSpec((1, tm, D_MODEL), row),
            pl.BlockSpec((1, m_len, 2 * D_MODEL), lambda bi, i: (bi, 0, 0)),
            pl.BlockSpec((D_MODEL, D_MODEL), lambda bi, i: (0, 0)),
            pl.BlockSpec((1, tm, D_MODEL), row),
            pl.BlockSpec((1, D_MODEL), lambda bi, i: (0, 0)),
            pl.BlockSpec((1, D_MODEL), lambda bi, i: (0, 0)),
        ],
        out_specs=pl.BlockSpec((1, tm, D_MODEL), row),
        out_shape=jax.ShapeDtypeStruct((b, s, D_MODEL), F32),
        scratch_shapes=[pltpu.VMEM((tm, D_MODEL), BF16)],
        compiler_params=_cparams(("parallel", "parallel")),
        name="xattn",
    )(q, kv, w_xo.astype(BF16), res, ln_g.reshape(1, -1), ln_b.reshape(1, -1))


def _in_proj_weights(w_in):
    qkv_end = 3 * A_WIDTH
    lat_end = qkv_end + Q_LORA + KV_LORA
    half = QK_ROPE // 2
    col_scale = jnp.concatenate([jnp.full((A_WIDTH,), A_HEAD_DIM ** -0.5, F32), jnp.ones((2 * A_WIDTH,), F32)])
    w_qkv = (w_in[:, :qkv_end] * col_scale).astype(BF16)
    x1 = w_in[:, lat_end:lat_end + half]
    x2 = w_in[:, lat_end + half:lat_end + QK_ROPE]
    kpe_a, kpe_b = _rot_pair(x1, x2, (D_MODEL, LANE - QK_ROPE))
    w_lat = jnp.concatenate([w_in[:, qkv_end:lat_end], kpe_a, kpe_b], axis=1).astype(BF16)
    w_gate = w_in[:, lat_end + QK_ROPE:].astype(BF16)
    return w_qkv, w_lat, w_gate


def kernel(x, mem, ffn1_w_in, ffn1_w_out, ln_ffn1_g, ln_ffn1_b, w_in, gate_bias, rel_bias, q_a_norm, w_q_b,
           kv_a_norm, w_kv_b, w_o_a, w_o_b, w_out, ln_mix_g, ln_mix_b, mem_ln_g, mem_ln_b, w_xq, w_xkv, w_xo,
           ln_x_g, ln_x_b, ffn2_w_in, ffn2_w_out, ln_ffn2_g, ln_ffn2_b):
    b, s, d = x.shape
    m = b * s
    h = x.reshape(m, d)
    for l in range(DEPTH):
        h = _ffn_ln(h, ffn1_w_in[l], ffn1_w_out[l], ln_ffn1_g[l], ln_ffn1_b[l])

        w_qkv, w_lat, w_gate = _in_proj_weights(w_in[l])
        qkv = _matmul(h, w_qkv, out_dtype=BF16, tn=1024, name="in_proj_qkv")
        gates = _matmul(h, w_gate, out_dtype=BF16, tn=1024, bias=gate_bias[l], act="sigmoid", name="in_proj_gates")
        lat = _matmul(h, w_lat, out_dtype=F32, tn=LAT_COLS, name="in_proj_latent")
        o_a = _attn_a(qkv.reshape(b, s, 3 * A_WIDTH), rel_bias[l])
        q_nope, q_pe, k_nope, v_t, k_pe = _latent_up(lat, q_a_norm[l], w_q_b[l], kv_a_norm[l], w_kv_b[l], s)
        o_b = _attn_b(q_nope.reshape(b, s, B_WIDTH), q_pe.reshape(b, s, B_WIDTH),
                      k_nope.reshape(b, s, B_WIDTH), k_pe.reshape(b, s, LANE), v_t)
        z = _gate_mix(o_a.reshape(m, A_WIDTH), o_b.reshape(m, B_WIDTH), gates, w_o_a[l], w_o_b[l])
        h = _proj_res_ln(z, w_out[l], h, ln_mix_g[l], ln_mix_b[l])

        m_len = mem.shape[1]
        kv_x = _matmul(mem.reshape(b * m_len, d), w_xkv[l].astype(BF16), out_dtype=BF16, tn=1024,
                       pro="ln", pro_g=mem_ln_g[l], pro_b=mem_ln_b[l], name="mem_kv")
        q_x = _matmul(h, w_xq[l].astype(BF16), out_dtype=BF16, tn=1024, out_scale=X_HEAD_DIM ** -0.5, name="xattn_q")
        h = _xattn(q_x.reshape(b, s, d), kv_x.reshape(b, m_len, 2 * d), w_xo[l], h.reshape(b, s, d),
                   ln_x_g[l], ln_x_b[l]).reshape(m, d)

        h = _ffn_ln(h, ffn2_w_in[l], ffn2_w_out[l], ln_ffn2_g[l], ln_ffn2_b[l])
    return h.reshape(b, s, d)
```

```python
import functools
import math

import jax
import jax.numpy as jnp
from jax import lax
from jax.experimental import pallas as pl
from jax.experimental.pallas import tpu as pltpu

BF16 = jnp.bfloat16
F32 = jnp.float32

D_MODEL = 2048
CHUNK = 64
LEFT_CHUNKS = 8
MEM_LEN = 256
A_HEADS = 16
A_HEAD_DIM = 64
A_WIDTH = A_HEADS * A_HEAD_DIM
MAX_REL = 128
B_HEADS = 16
QK_NOPE = 128
QK_ROPE = 64
V_HEAD = 128
Q_LORA = 512
KV_LORA = 512
ROPE_THETA = 10000.0
B_WIDTH = B_HEADS * V_HEAD
X_HEADS = 4
X_HEAD_DIM = D_MODEL // X_HEADS
D_FF = 5504
DEPTH = 1
ALPHA = (2.0 * DEPTH) ** 0.25
LN_EPS = 1e-5
RMS_EPS = 1e-6
NEG_INF = -1e30
LOG2_E = math.log2(math.e)

LANE = 128
VMEM_LIMIT_BYTES = 56 * 1024 * 1024

FF_TILE = 512
D_FF_PAD = ((D_FF + FF_TILE - 1) // FF_TILE) * FF_TILE
ROW_TILE = 512
FFN_ROW_TILE = 512
MM_ROW_TILE = 1024
A_Q_TILE = 256
A_LEFT = LEFT_CHUNKS * CHUNK
A_WIN = A_LEFT + A_Q_TILE
A_TOEP = A_Q_TILE + A_WIN
B_TILE = 512
B_GROUP = 4
B_VT_ROWS = V_HEAD + 16
LAT_COLS = Q_LORA + KV_LORA + 2 * LANE


def _cparams(semantics):
    return pltpu.CompilerParams(dimension_semantics=semantics, vmem_limit_bytes=VMEM_LIMIT_BYTES)


def _layer_norm(x, g, b):
    mu = jnp.mean(x, axis=-1, keepdims=True)
    xc = x - mu
    var = jnp.mean(xc * xc, axis=-1, keepdims=True)
    return xc * lax.rsqrt(var + LN_EPS) * g + b


def _rms_norm(x, g):
    return x * lax.rsqrt(jnp.mean(x * x, axis=-1, keepdims=True) + RMS_EPS) * g


def _dot(a, b):
    return jnp.dot(a, b, preferred_element_type=F32)


def _dot_nt(a, b):
    return lax.dot_general(a, b, (((1,), (1,)), ((), ())), preferred_element_type=F32)


def _ffn_body(x_ref, wg_ref, wu_ref, wo_ref, g_ref, b_ref, o_ref, xb_ref):
    j = pl.program_id(1)

    @pl.when(j == 0)
    def _():
        xb_ref[...] = x_ref[...].astype(BF16)
        o_ref[...] = jnp.zeros_like(o_ref)

    xb = xb_ref[...]
    gate = _dot(xb, wg_ref[...])
    up = _dot(xb, wu_ref[...])
    act = (gate * jax.nn.sigmoid(gate)) * up
    o_ref[...] += _dot(act.astype(BF16), wo_ref[...])

    @pl.when(j == pl.num_programs(1) - 1)
    def _():
        y = ALPHA * x_ref[...] + 0.5 * o_ref[...]
        o_ref[...] = _layer_norm(y, g_ref[...], b_ref[...])


def _ffn_weights(w_in, w_out):
    pad = D_FF_PAD - D_FF
    col_pad = jnp.zeros((D_MODEL, pad), BF16)
    wg = jnp.concatenate([w_in[:, :D_FF].astype(BF16), col_pad], axis=1)
    wu = jnp.concatenate([w_in[:, D_FF:].astype(BF16), col_pad], axis=1)
    wo = jnp.concatenate([w_out.astype(BF16), jnp.zeros((pad, D_MODEL), BF16)], axis=0)
    return wg, wu, wo


def _ffn_ln(x, w_in, w_out, ln_g, ln_b):
    m = x.shape[0]
    tm = min(FFN_ROW_TILE, m)
    wg, wu, wo = _ffn_weights(w_in, w_out)
    return pl.pallas_call(
        _ffn_body,
        grid=(m // tm, D_FF_PAD // FF_TILE),
        in_specs=[
            pl.BlockSpec((tm, D_MODEL), lambda i, j: (i, 0)),
            pl.BlockSpec((D_MODEL, FF_TILE), lambda i, j: (0, j)),
            pl.BlockSpec((D_MODEL, FF_TILE), lambda i, j: (0, j)),
            pl.BlockSpec((FF_TILE, D_MODEL), lambda i, j: (j, 0)),
            pl.BlockSpec((1, D_MODEL), lambda i, j: (0, 0)),
            pl.BlockSpec((1, D_MODEL), lambda i, j: (0, 0)),
        ],
        out_specs=pl.BlockSpec((tm, D_MODEL), lambda i, j: (i, 0)),
        out_shape=jax.ShapeDtypeStruct((m, D_MODEL), F32),
        scratch_shapes=[pltpu.VMEM((tm, D_MODEL), BF16)],
        compiler_params=_cparams(("parallel", "arbitrary")),
        name="ffn_ln",
    )(x, wg, wu, wo, ln_g.reshape(1, -1), ln_b.reshape(1, -1))


def _mm_body(*refs, pro, has_bias, act, out_scale, has_col_scale):
    it = iter(refs)
    x_ref = next(it)
    pg_ref = next(it) if pro else None
    pb_ref = next(it) if pro == "ln" else None
    w_ref = next(it)
    bias_ref = next(it) if has_bias else None
    cs_ref = next(it) if has_col_scale else None
    o_ref = next(it)
    x = x_ref[...]
    if pro == "ln":
        x = _layer_norm(x.astype(F32), pg_ref[...], pb_ref[...])
    elif pro == "rms":
        x = _rms_norm(x.astype(F32), pg_ref[...])
    y = _dot(x.astype(BF16), w_ref[...])
    if has_bias:
        y = y + bias_ref[...]
    if act == "sigmoid":
        y = jax.nn.sigmoid(y)
    if out_scale is not None:
        y = y * out_scale
    if has_col_scale:
        y = y * cs_ref[...]
    o_ref[...] = y.astype(o_ref.dtype)


def _matmul(x, w, *, out_dtype, tn, x_cols=None, pro=None, pro_g=None, pro_b=None,
            bias=None, act=None, out_scale=None, col_scale=None, name="matmul"):
    m = x.shape[0]
    k, n = w.shape
    tm = min(MM_ROW_TILE, m)
    xblk = 0 if x_cols is None else x_cols
    args = [x]
    specs = [pl.BlockSpec((tm, k), lambda i, j: (i, xblk))]
    if pro:
        args.append(pro_g.reshape(1, k))
        specs.append(pl.BlockSpec((1, k), lambda i, j: (0, 0)))
    if pro == "ln":
        args.append(pro_b.reshape(1, k))
        specs.append(pl.BlockSpec((1, k), lambda i, j: (0, 0)))
    args.append(w)
    specs.append(pl.BlockSpec((k, tn), lambda i, j: (0, j)))
    if bias is not None:
        args.append(bias.reshape(1, n))
        specs.append(pl.BlockSpec((1, tn), lambda i, j: (0, j)))
    if col_scale is not None:
        args.append(col_scale.reshape(1, n))
        specs.append(pl.BlockSpec((1, tn), lambda i, j: (0, j)))
    body = functools.partial(_mm_body, pro=pro, has_bias=bias is not None, act=act, out_scale=out_scale,
                             has_col_scale=col_scale is not None)
    return pl.pallas_call(
        body,
        grid=(m // tm, n // tn),
        in_specs=specs,
        out_specs=pl.BlockSpec((tm, tn), lambda i, j: (i, j)),
        out_shape=jax.ShapeDtypeStruct((m, n), out_dtype),
        compiler_params=_cparams(("parallel", "parallel")),
        name=name,
    )(*args)


def _q_up_body(lat_ref, g_ref, w_ref, cos_ref, sin_ref, qn_ref, qp_ref, *, scale):
    c = _rms_norm(lat_ref[...], g_ref[...]).astype(BF16)
    qn_ref[...] = (_dot(c, w_ref[:, :B_WIDTH]) * scale).astype(BF16)
    pa = _dot(c, w_ref[:, B_WIDTH:2 * B_WIDTH])
    pb = _dot(c, w_ref[:, 2 * B_WIDTH:])
    cos = cos_ref[...] * scale
    sin = sin_ref[...] * scale
    for h in range(B_HEADS):
        sl = slice(h * LANE, (h + 1) * LANE)
        qp_ref[:, sl] = (pa[:, sl] * cos + pb[:, sl] * sin).astype(BF16)


def _kv_up_body(c_ref, g_ref, wk_ref, wvt_ref, ka_ref, kb_ref, cos_ref, sin_ref, kn_ref, vt_ref, kpe_ref):
    c = _rms_norm(c_ref[...], g_ref[...]).astype(BF16)
    n_chunk = 1024
    for n in range(B_WIDTH // n_chunk):
        sl = slice(n * n_chunk, (n + 1) * n_chunk)
        kn_ref[:, sl] = _dot(c, wk_ref[:, sl]).astype(BF16)
    heads_per_dot = 4
    tm = c.shape[0]
    ones_rows = (lax.broadcasted_iota(jnp.int32, (B_VT_ROWS - V_HEAD, tm), 0) == 0).astype(BF16)
    for hc in range(B_HEADS // heads_per_dot):
        rows = slice(hc * heads_per_dot * V_HEAD, (hc + 1) * heads_per_dot * V_HEAD)
        vt = _dot_nt(wvt_ref[rows, :], c).astype(BF16)
        for r in range(heads_per_dot):
            vt_ref[0, hc * heads_per_dot + r, 0, :V_HEAD, :] = vt[r * V_HEAD:(r + 1) * V_HEAD, :]
            vt_ref[0, hc * heads_per_dot + r, 0, V_HEAD:, :] = ones_rows
    kpe_ref[...] = (ka_ref[...] * cos_ref[...] + kb_ref[...] * sin_ref[...]).astype(BF16)


def _rope_tables(seq_len):
    half = QK_ROPE // 2
    inv = 1.0 / (ROPE_THETA ** (jnp.arange(0, QK_ROPE, 2, dtype=F32) / QK_ROPE))
    ang = jnp.arange(seq_len, dtype=F32)[:, None] * inv[None, :]
    zeros = jnp.zeros((seq_len, LANE - 2 * half), F32)
    cos = jnp.concatenate([jnp.cos(ang), jnp.cos(ang), zeros], axis=1)
    sin = jnp.concatenate([jnp.sin(ang), jnp.sin(ang), zeros], axis=1)
    return cos, sin


def _rot_pair(x1, x2, pad_shape):
    zeros = jnp.zeros(pad_shape, x1.dtype)
    return (jnp.concatenate([x1, x2, zeros], axis=-1), jnp.concatenate([-x2, x1, zeros], axis=-1))


def _latent_up(lat, q_a_norm, w_q_b, kv_a_norm, w_kv_b, seq_len):
    m = lat.shape[0]
    tm = min(B_TILE, seq_len)
    n_pos = seq_len // tm
    cos, sin = _rope_tables(seq_len)
    half = QK_ROPE // 2
    wq = w_q_b.reshape(Q_LORA, B_HEADS, QK_NOPE + QK_ROPE)
    pe = wq[:, :, QK_NOPE:]
    pe_a, pe_b = _rot_pair(pe[..., :half], pe[..., half:], (Q_LORA, B_HEADS, LANE - QK_ROPE))
    w_q_all = jnp.concatenate([wq[:, :, :QK_NOPE].reshape(Q_LORA, B_WIDTH),
                               pe_a.reshape(Q_LORA, B_WIDTH), pe_b.reshape(Q_LORA, B_WIDTH)],
                              axis=1).astype(BF16)
    wkv = w_kv_b.reshape(KV_LORA, B_HEADS, QK_NOPE + V_HEAD)
    w_k = wkv[:, :, :QK_NOPE].reshape(KV_LORA, B_WIDTH).astype(BF16)
    w_vt = wkv[:, :, QK_NOPE:].reshape(KV_LORA, B_WIDTH).T.astype(BF16)
    scale = (QK_NOPE + QK_ROPE) ** -0.5 * math.log2(math.e)
    row = lambda i: (i, 0)
    const = lambda i: (0, 0)
    pos = lambda i: (i % n_pos, 0)
    q_nope, q_pe = pl.pallas_call(
        functools.partial(_q_up_body, scale=scale),
        grid=(m // tm,),
        in_specs=[
            pl.BlockSpec((tm, Q_LORA), row),
            pl.BlockSpec((1, Q_LORA), const),
            pl.BlockSpec((Q_LORA, 3 * B_WIDTH), const),
            pl.BlockSpec((tm, LANE), pos),
            pl.BlockSpec((tm, LANE), pos),
        ],
        out_specs=[pl.BlockSpec((tm, B_WIDTH), row), pl.BlockSpec((tm, B_WIDTH), row)],
        out_shape=[jax.ShapeDtypeStruct((m, B_WIDTH), BF16), jax.ShapeDtypeStruct((m, B_WIDTH), BF16)],
        compiler_params=_cparams(("parallel",)),
        name="q_up",
    )(lat, q_a_norm.reshape(1, -1), w_q_all, cos, sin)
    ka_blk = (Q_LORA + KV_LORA) // LANE
    batch = m // seq_len
    k_nope, v_t, k_pe = pl.pallas_call(
        _kv_up_body,
        grid=(m // tm,),
        in_specs=[
            pl.BlockSpec((tm, KV_LORA), lambda i: (i, Q_LORA // KV_LORA)),
            pl.BlockSpec((1, KV_LORA), const),
            pl.BlockSpec((KV_LORA, B_WIDTH), const),
            pl.BlockSpec((B_WIDTH, KV_LORA), const),
            pl.BlockSpec((tm, LANE), lambda i: (i, ka_blk)),
            pl.BlockSpec((tm, LANE), lambda i: (i, ka_blk + 1)),
            pl.BlockSpec((tm, LANE), pos),
            pl.BlockSpec((tm, LANE), pos),
        ],
        out_specs=[pl.BlockSpec((tm, B_WIDTH), row),
                   pl.BlockSpec((1, B_HEADS, 1, B_VT_ROWS, tm), lambda i: (i // n_pos, 0, i % n_pos, 0, 0)),
                   pl.BlockSpec((tm, LANE), row)],
        out_shape=[jax.ShapeDtypeStruct((m, B_WIDTH), BF16),
                   jax.ShapeDtypeStruct((batch, B_HEADS, n_pos, B_VT_ROWS, tm), BF16),
                   jax.ShapeDtypeStruct((m, LANE), BF16)],
        compiler_params=_cparams(("parallel",)),
        name="kv_up",
    )(lat, kv_a_norm.reshape(1, -1), w_k, w_vt, lat, lat, cos, sin)
    return q_nope, q_pe, k_nope, v_t, k_pe


def _attn_a_body(q_ref, k0_ref, k1_ref, k2_ref, v0_ref, v1_ref, v2_ref, toep_ref, o_ref, bias_ref):
    i = pl.program_id(1)

    @pl.when((pl.program_id(0) == 0) & (i == 0))
    def _():
        shift = int(math.log2(CHUNK))
        qc = lax.broadcasted_iota(jnp.int32, (A_Q_TILE, A_WIN), 0) >> shift
        kc = lax.broadcasted_iota(jnp.int32, (A_Q_TILE, A_WIN), 1) >> shift
        band = (kc >= qc) & (kc <= qc + LEFT_CHUNKS)
        for h in range(A_HEADS):
            rows = jnp.broadcast_to(toep_ref[h:h + 1, :], (A_Q_TILE, A_TOEP))
            tile = pltpu.roll(rows, 0, 1, stride=1, stride_axis=0)
            bias_ref[h] = jnp.where(band, tile[:, :A_WIN] * LOG2_E, NEG_INF)

    col = lax.broadcasted_iota(jnp.int32, (A_Q_TILE, A_WIN), 1)
    in_seq = col >= (A_LEFT // A_Q_TILE - i) * A_Q_TILE
    lane = lax.broadcasted_iota(jnp.int32, (1, LANE), 1)
    low = lane < A_HEAD_DIM
    ones = jnp.ones((A_WIN, LANE), BF16)
    for hp in range(A_WIDTH // LANE):
        sl = slice(hp * LANE, (hp + 1) * LANE)
        qp = q_ref[0, :, sl]
        kp = jnp.concatenate([k0_ref[0, :, sl], k1_ref[0, :, sl], k2_ref[0, :, sl]], axis=0)
        vp = jnp.concatenate([v0_ref[0, :, sl], v1_ref[0, :, sl], v2_ref[0, :, sl]], axis=0)
        vp_ones = jnp.concatenate([vp, ones], axis=1)
        outs = []
        for sub in range(2):
            sel = low if sub == 0 else jnp.logical_not(low)
            qm = jnp.where(sel, qp, jnp.zeros_like(qp))
            s = _dot_nt(qm, kp)
            s = jnp.where(in_seq, s + bias_ref[2 * hp + sub], NEG_INF)
            mx = jnp.max(s, axis=-1, keepdims=True)
            e = jnp.exp2(s - mx).astype(BF16)
            o_den = _dot(e, vp_ones)
            outs.append(o_den[:, :LANE] / o_den[:, LANE:])
        o_ref[0, :, sl] = jnp.where(low, outs[0], outs[1]).astype(o_ref.dtype)


def _attn_a_toeplitz_row(rel_bias):
    n_head = rel_bias.shape[0]
    far_left = jnp.broadcast_to(rel_bias[:, -1:], (n_head, A_LEFT - MAX_REL))
    far_right = jnp.broadcast_to(rel_bias[:, :1], (n_head, A_WIN - A_LEFT - MAX_REL))
    wrapped = jnp.broadcast_to(rel_bias[:, -1:], (n_head, A_Q_TILE - 1))
    return jnp.concatenate([far_left, rel_bias[:, ::-1], far_right, wrapped], axis=1).astype(F32)


def _attn_a(qkv, rel_bias):
    b, s, _ = qkv.shape
    nq = s // A_Q_TILE
    back = A_LEFT // A_Q_TILE
    blk = (1, A_Q_TILE, A_WIDTH)

    def kv_spec(col, off):
        return pl.BlockSpec(blk, lambda bi, i: (bi, jnp.maximum(i - back + off, 0), col))

    return pl.pallas_call(
        _attn_a_body,
        grid=(b, nq),
        in_specs=[pl.BlockSpec(blk, lambda bi, i: (bi, i, 0))]
        + [kv_spec(1, off) for off in range(back + 1)]
        + [kv_spec(2, off) for off in range(back + 1)]
        + [pl.BlockSpec((A_HEADS, A_TOEP), lambda bi, i: (0, 0))],
        out_specs=pl.BlockSpec(blk, lambda bi, i: (bi, i, 0)),
        out_shape=jax.ShapeDtypeStruct((b, s, A_WIDTH), BF16),
        scratch_shapes=[pltpu.VMEM((A_HEADS, A_Q_TILE, A_WIN), F32)],
        compiler_params=_cparams(("arbitrary", "arbitrary")),
        name="attn_a",
    )(qkv, qkv, qkv, qkv, qkv, qkv, qkv, _attn_a_toeplitz_row(rel_bias))


def _attn_b_body(qn_ref, qp_ref, kn_ref, kpe_ref, vt_ref, o_ref, m_ref, acc_ref, s_ref):
    qi = pl.program_id(2)
    t = qn_ref.shape[1]
    slabs = [slice(g * LANE, (g + 1) * LANE) for g in range(B_GROUP)]
    qs = [jnp.concatenate([qn_ref[0, :, sl], qp_ref[0, :, sl]], axis=-1) for sl in slabs]
    shift = int(math.log2(CHUNK))
    key_chunk = lax.broadcasted_iota(jnp.int32, (t, t), 0) >> shift
    qry_chunk = lax.broadcasted_iota(jnp.int32, (t, t), 1) >> shift
    diag_allowed = key_chunk <= qry_chunk

    m_ref[...] = jnp.full_like(m_ref, NEG_INF)
    acc_ref[...] = jnp.zeros_like(acc_ref)

    def scores_into(j, slot):
        rows = pl.ds(pl.multiple_of(j * t, t), t)
        kpe = kpe_ref[0, rows, :]
        for g, sl in enumerate(slabs):
            s_ref[slot, g] = _dot_nt(jnp.concatenate([kn_ref[0, rows, sl], kpe], axis=-1), qs[g])

    def consume(j, slot, masked):
        for g in range(B_GROUP):
            st = s_ref[slot, g]
            if masked:
                st = jnp.where(diag_allowed, st, NEG_INF)
            m_old = m_ref[g]
            m_new = jnp.maximum(m_old, jnp.max(st, axis=0, keepdims=True))
            corr = jnp.exp2(m_old - m_new)
            p = jnp.exp2(st - m_new).astype(BF16)
            m_ref[g] = m_new
            acc_ref[g] = corr * acc_ref[g] + _dot(vt_ref[0, g, j], p)

    scores_into(0, 0)

    def pair_step(p, carry):
        j = 2 * p
        scores_into(j + 1, 1)
        consume(j, 0, False)
        scores_into(j + 2, 0)
        consume(j + 1, 1, False)
        return carry

    lax.fori_loop(0, qi // 2, pair_step, 0)

    @pl.when(qi % 2 == 0)
    def _():
        consume(qi, 0, True)

    @pl.when(qi % 2 == 1)
    def _():
        scores_into(qi, 1)
        consume(qi - 1, 0, False)
        consume(qi, 1, True)
    for g, sl in enumerate(slabs):
        o_ref[0, :, sl] = (acc_ref[g, :V_HEAD, :] / acc_ref[g, V_HEAD:V_HEAD + 1, :]).T.astype(o_ref.dtype)


def _attn_b(q_nope, q_pe, k_nope, k_pe, v_t):
    b, s, _ = q_nope.shape
    t = v_t.shape[-1]
    width = B_GROUP * LANE
    qblk = (1, t, width)
    return pl.pallas_call(
        _attn_b_body,
        grid=(b, B_HEADS // B_GROUP, s // t),
        in_specs=[
            pl.BlockSpec(qblk, lambda bi, hg, i: (bi, i, hg)),
            pl.BlockSpec(qblk, lambda bi, hg, i: (bi, i, hg)),
            pl.BlockSpec((1, s, width), lambda bi, hg, i: (bi, 0, hg)),
            pl.BlockSpec((1, s, LANE), lambda bi, hg, i: (bi, 0, 0)),
            pl.BlockSpec((1, B_GROUP, s // t, B_VT_ROWS, t), lambda bi, hg, i: (bi, hg, 0, 0, 0)),
        ],
        out_specs=pl.BlockSpec(qblk, lambda bi, hg, i: (bi, i, hg)),
        out_shape=jax.ShapeDtypeStruct((b, s, B_WIDTH), BF16),
        scratch_shapes=[pltpu.VMEM((B_GROUP, 1, t), F32), pltpu.VMEM((B_GROUP, B_VT_ROWS, t), F32),
                        pltpu.VMEM((2, B_GROUP, t, t), F32)],
        compiler_params=_cparams(("parallel", "parallel", "arbitrary")),
        name="attn_b",
    )(q_nope, q_pe, k_nope, k_pe, v_t)


def _gate_mix_body(oa_ref, ob_ref, ga_ref, gb_ref, wa_ref, wb_ref, z_ref):
    ya = _dot(oa_ref[...], wa_ref[...])
    yb = _dot(ob_ref[...], wb_ref[...])
    z_ref[...] = (ga_ref[...].astype(F32) * ya + gb_ref[...].astype(F32) * yb).astype(z_ref.dtype)


def _gate_mix(o_a, o_b, gates, w_o_a, w_o_b):
    m = o_a.shape[0]
    tm = min(ROW_TILE, m)
    tn = 1024
    nj = D_MODEL // tn
    return pl.pallas_call(
        _gate_mix_body,
        grid=(m // tm, nj),
        in_specs=[
            pl.BlockSpec((tm, A_WIDTH), lambda i, j: (i, 0)),
            pl.BlockSpec((tm, B_WIDTH), lambda i, j: (i, 0)),
            pl.BlockSpec((tm, tn), lambda i, j: (i, j)),
            pl.BlockSpec((tm, tn), lambda i, j: (i, nj + j)),
            pl.BlockSpec((A_WIDTH, tn), lambda i, j: (0, j)),
            pl.BlockSpec((B_WIDTH, tn), lambda i, j: (0, j)),
        ],
        out_specs=pl.BlockSpec((tm, tn), lambda i, j: (i, j)),
        out_shape=jax.ShapeDtypeStruct((m, D_MODEL), BF16),
        compiler_params=_cparams(("parallel", "parallel")),
        name="gate_mix",
    )(o_a, o_b, gates, gates, w_o_a.astype(BF16), w_o_b.astype(BF16))


def _proj_res_ln_body(x_ref, w_ref, res_ref, g_ref, b_ref, o_ref):
    y = ALPHA * res_ref[...] + _dot(x_ref[...], w_ref[...])
    o_ref[...] = _layer_norm(y, g_ref[...], b_ref[...])


def _proj_res_ln(x, w, res, ln_g, ln_b):
    m, k = x.shape
    tm = min(ROW_TILE, m)
    row = lambda i: (i, 0)
    const = lambda i: (0, 0)
    return pl.pallas_call(
        _proj_res_ln_body,
        grid=(m // tm,),
        in_specs=[
            pl.BlockSpec((tm, k), row),
            pl.BlockSpec((k, D_MODEL), const),
            pl.BlockSpec((tm, D_MODEL), row),
            pl.BlockSpec((1, D_MODEL), const),
            pl.BlockSpec((1, D_MODEL), const),
        ],
        out_specs=pl.BlockSpec((tm, D_MODEL), row),
        out_shape=jax.ShapeDtypeStruct((m, D_MODEL), F32),
        compiler_params=_cparams(("parallel",)),
        name="proj_res_ln",
    )(x, w.astype(BF16), res, ln_g.reshape(1, -1), ln_b.reshape(1, -1))


def _xattn_body(q_ref, kv_ref, w_ref, res_ref, g_ref, b_ref, o_ref, att_ref):
    for h in range(X_HEADS):
        sl = slice(h * X_HEAD_DIM, (h + 1) * X_HEAD_DIM)
        vs = slice(D_MODEL + h * X_HEAD_DIM, D_MODEL + (h + 1) * X_HEAD_DIM)
        s = _dot_nt(q_ref[0, :, sl], kv_ref[0, :, sl])
        mx = jnp.max(s, axis=-1, keepdims=True)
        e = jnp.exp(s - mx)
        den = jnp.sum(e, axis=-1, keepdims=True)
        att_ref[:, sl] = (_dot(e.astype(BF16), kv_ref[0, :, vs]) / den).astype(BF16)
    y = ALPHA * res_ref[0] + _dot(att_ref[...], w_ref[...])
    o_ref[0] = _layer_norm(y, g_ref[...], b_ref[...])


def _xattn(q, kv, w_xo, res, ln_g, ln_b):
    b, s, _ = q.shape
    tm = min(ROW_TILE, s)
    m_len = kv.shape[1]
    row = lambda bi, i: (bi, i, 0)
    return pl.pallas_call(
        _xattn_body,
        grid=(b, s // tm),
        in_specs=[
            pl.BlockSpec((1, tm, D_MODEL), row),
            pl.BlockSpec((1, m_len, 2 * D_MODEL), lambda bi, i: (bi, 0, 0)),
            pl.BlockSpec((D_MODEL, D_MODEL), lambda bi, i: (0, 0)),
            pl.BlockSpec((1, tm, D_MODEL), row),
            pl.BlockSpec((1, D_MODEL), lambda bi, i: (0, 0)),
            pl.BlockSpec((1, D_MODEL), lambda bi, i: (0, 0)),
        ],
        out_specs=pl.BlockSpec((1, tm, D_MODEL), row),
        out_shape=jax.ShapeDtypeStruct((b, s, D_MODEL), F32),
        scratch_shapes=[pltpu.VMEM((tm, D_MODEL), BF16)],
        compiler_params=_cparams(("parallel", "parallel")),
        name="xattn",
    )(q, kv, w_xo.astype(BF16), res, ln_g.reshape(1, -1), ln_b.reshape(1, -1))


def _in_proj_weights(w_in):
    qkv_end = 3 * A_WIDTH
    lat_end = qkv_end + Q_LORA + KV_LORA
    half = QK_ROPE // 2
    w_qkv = w_in[:, :qkv_end].astype(BF16)
    x1 = w_in[:, lat_end:lat_end + half]
    x2 = w_in[:, lat_end + half:lat_end + QK_ROPE]
    kpe_a, kpe_b = _rot_pair(x1, x2, (D_MODEL, LANE - QK_ROPE))
    w_lat = jnp.concatenate([w_in[:, qkv_end:lat_end], kpe_a, kpe_b], axis=1).astype(BF16)
    w_gate = w_in[:, lat_end + QK_ROPE:].astype(BF16)
    return w_qkv, w_lat, w_gate


def kernel(x, mem, ffn1_w_in, ffn1_w_out, ln_ffn1_g, ln_ffn1_b, w_in, gate_bias, rel_bias, q_a_norm, w_q_b,
           kv_a_norm, w_kv_b, w_o_a, w_o_b, w_out, ln_mix_g, ln_mix_b, mem_ln_g, mem_ln_b, w_xq, w_xkv, w_xo,
           ln_x_g, ln_x_b, ffn2_w_in, ffn2_w_out, ln_ffn2_g, ln_ffn2_b):
    b, s, d = x.shape
    m = b * s
    h = x.reshape(m, d)
    for l in range(DEPTH):
        h = _ffn_ln(h, ffn1_w_in[l], ffn1_w_out[l], ln_ffn1_g[l], ln_ffn1_b[l])

        w_qkv, w_lat, w_gate = _in_proj_weights(w_in[l])
        qkv_scale = jnp.concatenate([jnp.full((A_WIDTH,), A_HEAD_DIM ** -0.5 * LOG2_E, F32),
                                     jnp.ones((2 * A_WIDTH,), F32)])
        qkv = _matmul(h, w_qkv, out_dtype=BF16, tn=1024, col_scale=qkv_scale, name="in_proj_qkv")
        gates = _matmul(h, w_gate, out_dtype=BF16, tn=1024, bias=gate_bias[l], act="sigmoid", name="in_proj_gates")
        lat = _matmul(h, w_lat, out_dtype=F32, tn=LAT_COLS, name="in_proj_latent")
        o_a = _attn_a(qkv.reshape(b, s, 3 * A_WIDTH), rel_bias[l])
        q_nope, q_pe, k_nope, v_t, k_pe = _latent_up(lat, q_a_norm[l], w_q_b[l], kv_a_norm[l], w_kv_b[l], s)
        o_b = _attn_b(q_nope.reshape(b, s, B_WIDTH), q_pe.reshape(b, s, B_WIDTH),
                      k_nope.reshape(b, s, B_WIDTH), k_pe.reshape(b, s, LANE), v_t)
        z = _gate_mix(o_a.reshape(m, A_WIDTH), o_b.reshape(m, B_WIDTH), gates, w_o_a[l], w_o_b[l])
        h = _proj_res_ln(z, w_out[l], h, ln_mix_g[l], ln_mix_b[l])

        m_len = mem.shape[1]
        kv_x = _matmul(mem.reshape(b * m_len, d), w_xkv[l].astype(BF16), out_dtype=BF16, tn=1024,
                       pro="ln", pro_g=mem_ln_g[l], pro_b=mem_ln_b[l], name="mem_kv")
        q_x = _matmul(h, w_xq[l].astype(BF16), out_dtype=BF16, tn=1024, out_scale=X_HEAD_DIM ** -0.5, name="xattn_q")
        h = _xattn(q_x.reshape(b, s, d), kv_x.reshape(b, m_len, 2 * d), w_xo[l], h.reshape(b, s, d),
                   ln_x_g[l], ln_x_b[l]).reshape(m, d)

        h = _ffn_ln(h, ffn2_w_in[l], ffn2_w_out[l], ln_ffn2_g[l], ln_ffn2_b[l])
    return h.reshape(b, s, d)
```

```python
import functools
import math

import jax
import jax.numpy as jnp
from jax import lax
from jax.experimental import pallas as pl
from jax.experimental.pallas import tpu as pltpu

BF16 = jnp.bfloat16
F32 = jnp.float32

D_MODEL = 2048
CHUNK = 64
LEFT_CHUNKS = 8
MEM_LEN = 256
A_HEADS = 16
A_HEAD_DIM = 64
A_WIDTH = A_HEADS * A_HEAD_DIM
MAX_REL = 128
B_HEADS = 16
QK_NOPE = 128
QK_ROPE = 64
V_HEAD = 128
Q_LORA = 512
KV_LORA = 512
ROPE_THETA = 10000.0
B_WIDTH = B_HEADS * V_HEAD
X_HEADS = 4
X_HEAD_DIM = D_MODEL // X_HEADS
D_FF = 5504
DEPTH = 1
ALPHA = (2.0 * DEPTH) ** 0.25
LN_EPS = 1e-5
RMS_EPS = 1e-6
NEG_INF = -1e30
LOG2_E = math.log2(math.e)

LANE = 128
VMEM_LIMIT_BYTES = 56 * 1024 * 1024

FF_TILE = 512
D_FF_PAD = ((D_FF + FF_TILE - 1) // FF_TILE) * FF_TILE
ROW_TILE = 512
FFN_ROW_TILE = 512
MM_ROW_TILE = 1024
A_Q_TILE = 256
A_LEFT = LEFT_CHUNKS * CHUNK
A_WIN = A_LEFT + A_Q_TILE
A_TOEP = A_Q_TILE + A_WIN
B_TILE = 512
B_GROUP = 4
B_VT_ROWS = V_HEAD + 16
LAT_COLS = Q_LORA + KV_LORA + 2 * LANE


def _cparams(semantics):
    return pltpu.CompilerParams(dimension_semantics=semantics, vmem_limit_bytes=VMEM_LIMIT_BYTES)


def _layer_norm(x, g, b):
    mu = jnp.mean(x, axis=-1, keepdims=True)
    xc = x - mu
    var = jnp.mean(xc * xc, axis=-1, keepdims=True)
    return xc * lax.rsqrt(var + LN_EPS) * g + b


def _rms_norm(x, g):
    return x * lax.rsqrt(jnp.mean(x * x, axis=-1, keepdims=True) + RMS_EPS) * g


def _dot(a, b):
    return jnp.dot(a, b, preferred_element_type=F32)


def _dot_nt(a, b):
    return lax.dot_general(a, b, (((1,), (1,)), ((), ())), preferred_element_type=F32)


def _ffn_body(x_ref, wg_ref, *rest):
    up_refs = rest[:FF_TILE // LANE]
    wo_ref, g_ref, b_ref, o_ref, xb_ref = rest[FF_TILE // LANE:]
    j = pl.program_id(1)

    @pl.when(j == 0)
    def _():
        xb_ref[...] = x_ref[...].astype(BF16)
        o_ref[...] = jnp.zeros_like(o_ref)

    xb = xb_ref[...]
    gate = _dot(xb, wg_ref[...])
    up = _dot(xb, jnp.concatenate([r[...] for r in up_refs], axis=1))
    act = (gate * jax.nn.sigmoid(gate)) * up
    o_ref[...] += _dot(act.astype(BF16), wo_ref[...])

    @pl.when(j == pl.num_programs(1) - 1)
    def _():
        y = ALPHA * x_ref[...] + 0.5 * o_ref[...]
        o_ref[...] = _layer_norm(y, g_ref[...], b_ref[...])


def _ffn_ln(x, w_in, w_out, ln_g, ln_b):
    m = x.shape[0]
    tm = min(FFN_ROW_TILE, m)
    w_gu = w_in.astype(BF16)
    wo = jnp.concatenate([w_out.astype(BF16), jnp.zeros((D_FF_PAD - D_FF, D_MODEL), BF16)], axis=0)
    slabs = FF_TILE // LANE
    up_first = D_FF // LANE
    last_slab = 2 * D_FF // LANE - 1

    def up_spec(r):
        return pl.BlockSpec((D_MODEL, LANE), lambda i, j: (0, jnp.minimum(up_first + slabs * j + r, last_slab)))

    return pl.pallas_call(
        _ffn_body,
        grid=(m // tm, D_FF_PAD // FF_TILE),
        in_specs=[
            pl.BlockSpec((tm, D_MODEL), lambda i, j: (i, 0)),
            pl.BlockSpec((D_MODEL, FF_TILE), lambda i, j: (0, j)),
        ] + [up_spec(r) for r in range(slabs)] + [
            pl.BlockSpec((FF_TILE, D_MODEL), lambda i, j: (j, 0)),
            pl.BlockSpec((1, D_MODEL), lambda i, j: (0, 0)),
            pl.BlockSpec((1, D_MODEL), lambda i, j: (0, 0)),
        ],
        out_specs=pl.BlockSpec((tm, D_MODEL), lambda i, j: (i, 0)),
        out_shape=jax.ShapeDtypeStruct((m, D_MODEL), F32),
        scratch_shapes=[pltpu.VMEM((tm, D_MODEL), BF16)],
        compiler_params=_cparams(("parallel", "arbitrary")),
        name="ffn_ln",
    )(x, w_gu, *([w_gu] * slabs), wo, ln_g.reshape(1, -1), ln_b.reshape(1, -1))


def _mm_body(*refs, pro, has_bias, act, out_scale, has_col_scale):
    it = iter(refs)
    x_ref = next(it)
    pg_ref = next(it) if pro else None
    pb_ref = next(it) if pro == "ln" else None
    w_ref = next(it)
    bias_ref = next(it) if has_bias else None
    cs_ref = next(it) if has_col_scale else None
    o_ref = next(it)
    x = x_ref[...]
    if pro == "ln":
        x = _layer_norm(x.astype(F32), pg_ref[...], pb_ref[...])
    elif pro == "rms":
        x = _rms_norm(x.astype(F32), pg_ref[...])
    y = _dot(x.astype(BF16), w_ref[...])
    if has_bias:
        y = y + bias_ref[...]
    if act == "sigmoid":
        y = jax.nn.sigmoid(y)
    if out_scale is not None:
        y = y * out_scale
    if has_col_scale:
        y = y * cs_ref[...]
    o_ref[...] = y.astype(o_ref.dtype)


def _matmul(x, w, *, out_dtype, tn, x_cols=None, pro=None, pro_g=None, pro_b=None,
            bias=None, act=None, out_scale=None, col_scale=None, name="matmul"):
    m = x.shape[0]
    k, n = w.shape
    tm = min(MM_ROW_TILE, m)
    xblk = 0 if x_cols is None else x_cols
    args = [x]
    specs = [pl.BlockSpec((tm, k), lambda i, j: (i, xblk))]
    if pro:
        args.append(pro_g.reshape(1, k))
        specs.append(pl.BlockSpec((1, k), lambda i, j: (0, 0)))
    if pro == "ln":
        args.append(pro_b.reshape(1, k))
        specs.append(pl.BlockSpec((1, k), lambda i, j: (0, 0)))
    args.append(w)
    specs.append(pl.BlockSpec((k, tn), lambda i, j: (0, j)))
    if bias is not None:
        args.append(bias.reshape(1, n))
        specs.append(pl.BlockSpec((1, tn), lambda i, j: (0, j)))
    if col_scale is not None:
        args.append(col_scale.reshape(1, n))
        specs.append(pl.BlockSpec((1, tn), lambda i, j: (0, j)))
    body = functools.partial(_mm_body, pro=pro, has_bias=bias is not None, act=act, out_scale=out_scale,
                             has_col_scale=col_scale is not None)
    return pl.pallas_call(
        body,
        grid=(m // tm, n // tn),
        in_specs=specs,
        out_specs=pl.BlockSpec((tm, tn), lambda i, j: (i, j)),
        out_shape=jax.ShapeDtypeStruct((m, n), out_dtype),
        compiler_params=_cparams(("parallel", "parallel")),
        name=name,
    )(*args)


def _q_up_body(lat_ref, g_ref, w_ref, cos_ref, sin_ref, qn_ref, qp_ref, *, scale):
    c = _rms_norm(lat_ref[...], g_ref[...]).astype(BF16)
    qn_ref[...] = (_dot(c, w_ref[:, :B_WIDTH]) * scale).astype(BF16)
    pa = _dot(c, w_ref[:, B_WIDTH:2 * B_WIDTH])
    pb = _dot(c, w_ref[:, 2 * B_WIDTH:])
    cos = cos_ref[...] * scale
    sin = sin_ref[...] * scale
    for h in range(B_HEADS):
        sl = slice(h * LANE, (h + 1) * LANE)
        qp_ref[:, sl] = (pa[:, sl] * cos + pb[:, sl] * sin).astype(BF16)


def _kv_up_body(c_ref, g_ref, wk_ref, wvt_ref, ka_ref, kb_ref, cos_ref, sin_ref, kn_ref, vt_ref, kpe_ref):
    c = _rms_norm(c_ref[...], g_ref[...]).astype(BF16)
    n_chunk = 1024
    for n in range(B_WIDTH // n_chunk):
        sl = slice(n * n_chunk, (n + 1) * n_chunk)
        kn_ref[:, sl] = _dot(c, wk_ref[:, sl]).astype(BF16)
    heads_per_dot = 4
    tm = c.shape[0]
    ones_rows = (lax.broadcasted_iota(jnp.int32, (B_VT_ROWS - V_HEAD, tm), 0) == 0).astype(BF16)
    for hc in range(B_HEADS // heads_per_dot):
        rows = slice(hc * heads_per_dot * V_HEAD, (hc + 1) * heads_per_dot * V_HEAD)
        vt = _dot_nt(wvt_ref[rows, :], c).astype(BF16)
        for r in range(heads_per_dot):
            vt_ref[0, hc * heads_per_dot + r, 0, :V_HEAD, :] = vt[r * V_HEAD:(r + 1) * V_HEAD, :]
            vt_ref[0, hc * heads_per_dot + r, 0, V_HEAD:, :] = ones_rows
    kpe_ref[...] = (ka_ref[...] * cos_ref[...] + kb_ref[...] * sin_ref[...]).astype(BF16)


def _rope_tables(seq_len):
    half = QK_ROPE // 2
    inv = 1.0 / (ROPE_THETA ** (jnp.arange(0, QK_ROPE, 2, dtype=F32) / QK_ROPE))
    ang = jnp.arange(seq_len, dtype=F32)[:, None] * inv[None, :]
    zeros = jnp.zeros((seq_len, LANE - 2 * half), F32)
    cos = jnp.concatenate([jnp.cos(ang), jnp.cos(ang), zeros], axis=1)
    sin = jnp.concatenate([jnp.sin(ang), jnp.sin(ang), zeros], axis=1)
    return cos, sin


def _rot_pair(x1, x2, pad_shape):
    zeros = jnp.zeros(pad_shape, x1.dtype)
    return (jnp.concatenate([x1, x2, zeros], axis=-1), jnp.concatenate([-x2, x1, zeros], axis=-1))


def _latent_up(lat, q_a_norm, w_q_b, kv_a_norm, w_kv_b, seq_len):
    m = lat.shape[0]
    tm = min(B_TILE, seq_len)
    n_pos = seq_len // tm
    cos, sin = _rope_tables(seq_len)
    half = QK_ROPE // 2
    wq = w_q_b.reshape(Q_LORA, B_HEADS, QK_NOPE + QK_ROPE)
    pe = wq[:, :, QK_NOPE:]
    pe_a, pe_b = _rot_pair(pe[..., :half], pe[..., half:], (Q_LORA, B_HEADS, LANE - QK_ROPE))
    w_q_all = jnp.concatenate([wq[:, :, :QK_NOPE].reshape(Q_LORA, B_WIDTH),
                               pe_a.reshape(Q_LORA, B_WIDTH), pe_b.reshape(Q_LORA, B_WIDTH)],
                              axis=1).astype(BF16)
    wkv = w_kv_b.reshape(KV_LORA, B_HEADS, QK_NOPE + V_HEAD)
    w_k = wkv[:, :, :QK_NOPE].reshape(KV_LORA, B_WIDTH).astype(BF16)
    w_vt = wkv[:, :, QK_NOPE:].reshape(KV_LORA, B_WIDTH).T.astype(BF16)
    scale = (QK_NOPE + QK_ROPE) ** -0.5 * math.log2(math.e)
    row = lambda i: (i, 0)
    const = lambda i: (0, 0)
    pos = lambda i: (i % n_pos, 0)
    q_nope, q_pe = pl.pallas_call(
        functools.partial(_q_up_body, scale=scale),
        grid=(m // tm,),
        in_specs=[
            pl.BlockSpec((tm, Q_LORA), row),
            pl.BlockSpec((1, Q_LORA), const),
            pl.BlockSpec((Q_LORA, 3 * B_WIDTH), const),
            pl.BlockSpec((tm, LANE), pos),
            pl.BlockSpec((tm, LANE), pos),
        ],
        out_specs=[pl.BlockSpec((tm, B_WIDTH), row), pl.BlockSpec((tm, B_WIDTH), row)],
        out_shape=[jax.ShapeDtypeStruct((m, B_WIDTH), BF16), jax.ShapeDtypeStruct((m, B_WIDTH), BF16)],
        compiler_params=_cparams(("parallel",)),
        name="q_up",
    )(lat, q_a_norm.reshape(1, -1), w_q_all, cos, sin)
    ka_blk = (Q_LORA + KV_LORA) // LANE
    batch = m // seq_len
    k_nope, v_t, k_pe = pl.pallas_call(
        _kv_up_body,
        grid=(m // tm,),
        in_specs=[
            pl.BlockSpec((tm, KV_LORA), lambda i: (i, Q_LORA // KV_LORA)),
            pl.BlockSpec((1, KV_LORA), const),
            pl.BlockSpec((KV_LORA, B_WIDTH), const),
            pl.BlockSpec((B_WIDTH, KV_LORA), const),
            pl.BlockSpec((tm, LANE), lambda i: (i, ka_blk)),
            pl.BlockSpec((tm, LANE), lambda i: (i, ka_blk + 1)),
            pl.BlockSpec((tm, LANE), pos),
            pl.BlockSpec((tm, LANE), pos),
        ],
        out_specs=[pl.BlockSpec((tm, B_WIDTH), row),
                   pl.BlockSpec((1, B_HEADS, 1, B_VT_ROWS, tm), lambda i: (i // n_pos, 0, i % n_pos, 0, 0)),
                   pl.BlockSpec((tm, LANE), row)],
        out_shape=[jax.ShapeDtypeStruct((m, B_WIDTH), BF16),
                   jax.ShapeDtypeStruct((batch, B_HEADS, n_pos, B_VT_ROWS, tm), BF16),
                   jax.ShapeDtypeStruct((m, LANE), BF16)],
        compiler_params=_cparams(("parallel",)),
        name="kv_up",
    )(lat, kv_a_norm.reshape(1, -1), w_k, w_vt, lat, lat, cos, sin)
    return q_nope, q_pe, k_nope, v_t, k_pe


def _attn_a_body(q_ref, k0_ref, k1_ref, k2_ref, v0_ref, v1_ref, v2_ref, toep_ref, o_ref, bias_ref):
    i = pl.program_id(1)

    @pl.when((pl.program_id(0) == 0) & (i == 0))
    def _():
        shift = int(math.log2(CHUNK))
        qc = lax.broadcasted_iota(jnp.int32, (A_Q_TILE, A_WIN), 0) >> shift
        kc = lax.broadcasted_iota(jnp.int32, (A_Q_TILE, A_WIN), 1) >> shift
        band = (kc >= qc) & (kc <= qc + LEFT_CHUNKS)
        for h in range(A_HEADS):
            rows = jnp.broadcast_to(toep_ref[h:h + 1, :], (A_Q_TILE, A_TOEP))
            tile = pltpu.roll(rows, 0, 1, stride=1, stride_axis=0)
            bias_ref[h] = jnp.where(band, tile[:, :A_WIN] * LOG2_E, NEG_INF)

    col = lax.broadcasted_iota(jnp.int32, (A_Q_TILE, A_WIN), 1)
    in_seq = col >= (A_LEFT // A_Q_TILE - i) * A_Q_TILE
    lane = lax.broadcasted_iota(jnp.int32, (1, LANE), 1)
    low = lane < A_HEAD_DIM
    ones = jnp.ones((A_WIN, LANE), BF16)
    for hp in range(A_WIDTH // LANE):
        sl = slice(hp * LANE, (hp + 1) * LANE)
        qp = q_ref[0, :, sl]
        kp = jnp.concatenate([k0_ref[0, :, sl], k1_ref[0, :, sl], k2_ref[0, :, sl]], axis=0)
        vp = jnp.concatenate([v0_ref[0, :, sl], v1_ref[0, :, sl], v2_ref[0, :, sl]], axis=0)
        vp_ones = jnp.concatenate([vp, ones], axis=1)
        outs = []
        for sub in range(2):
            sel = low if sub == 0 else jnp.logical_not(low)
            qm = jnp.where(sel, qp, jnp.zeros_like(qp))
            s = _dot_nt(qm, kp)
            s = jnp.where(in_seq, s + bias_ref[2 * hp + sub], NEG_INF)
            mx = jnp.max(s, axis=-1, keepdims=True)
            e = jnp.exp2(s - mx).astype(BF16)
            o_den = _dot(e, vp_ones)
            outs.append(o_den[:, :LANE] / o_den[:, LANE:])
        o_ref[0, :, sl] = jnp.where(low, outs[0], outs[1]).astype(o_ref.dtype)


def _attn_a_toeplitz_row(rel_bias):
    n_head = rel_bias.shape[0]
    far_left = jnp.broadcast_to(rel_bias[:, -1:], (n_head, A_LEFT - MAX_REL))
    far_right = jnp.broadcast_to(rel_bias[:, :1], (n_head, A_WIN - A_LEFT - MAX_REL))
    wrapped = jnp.broadcast_to(rel_bias[:, -1:], (n_head, A_Q_TILE - 1))
    return jnp.concatenate([far_left, rel_bias[:, ::-1], far_right, wrapped], axis=1).astype(F32)


def _attn_a(qkv, rel_bias):
    b, s, _ = qkv.shape
    nq = s // A_Q_TILE
    back = A_LEFT // A_Q_TILE
    blk = (1, A_Q_TILE, A_WIDTH)

    def kv_spec(col, off):
        return pl.BlockSpec(blk, lambda bi, i: (bi, jnp.maximum(i - back + off, 0), col))

    return pl.pallas_call(
        _attn_a_body,
        grid=(b, nq),
        in_specs=[pl.BlockSpec(blk, lambda bi, i: (bi, i, 0))]
        + [kv_spec(1, off) for off in range(back + 1)]
        + [kv_spec(2, off) for off in range(back + 1)]
        + [pl.BlockSpec((A_HEADS, A_TOEP), lambda bi, i: (0, 0))],
        out_specs=pl.BlockSpec(blk, lambda bi, i: (bi, i, 0)),
        out_shape=jax.ShapeDtypeStruct((b, s, A_WIDTH), BF16),
        scratch_shapes=[pltpu.VMEM((A_HEADS, A_Q_TILE, A_WIN), F32)],
        compiler_params=_cparams(("arbitrary", "arbitrary")),
        name="attn_a",
    )(qkv, qkv, qkv, qkv, qkv, qkv, qkv, _attn_a_toeplitz_row(rel_bias))


def _attn_b_body(qn_ref, qp_ref, kn_ref, kpe_ref, vt_ref, o_ref, m_ref, acc_ref, s_ref):
    qi = pl.program_id(2)
    t = qn_ref.shape[1]
    slabs = [slice(g * LANE, (g + 1) * LANE) for g in range(B_GROUP)]
    qs = [jnp.concatenate([qn_ref[0, :, sl], qp_ref[0, :, sl]], axis=-1) for sl in slabs]
    shift = int(math.log2(CHUNK))
    key_chunk = lax.broadcasted_iota(jnp.int32, (t, t), 0) >> shift
    qry_chunk = lax.broadcasted_iota(jnp.int32, (t, t), 1) >> shift
    diag_allowed = key_chunk <= qry_chunk

    m_ref[...] = jnp.full_like(m_ref, NEG_INF)
    acc_ref[...] = jnp.zeros_like(acc_ref)

    def scores_into(j, slot):
        rows = pl.ds(pl.multiple_of(j * t, t), t)
        kpe = kpe_ref[0, rows, :]
        for g, sl in enumerate(slabs):
            s_ref[slot, g] = _dot_nt(jnp.concatenate([kn_ref[0, rows, sl], kpe], axis=-1), qs[g])

    def consume(j, slot, masked):
        for g in range(B_GROUP):
            st = s_ref[slot, g]
            if masked:
                st = jnp.where(diag_allowed, st, NEG_INF)
            m_old = m_ref[g]
            m_new = jnp.maximum(m_old, jnp.max(st, axis=0, keepdims=True))
            corr = jnp.exp2(m_old - m_new)
            p = jnp.exp2(st - m_new).astype(BF16)
            m_ref[g] = m_new
            acc_ref[g] = corr * acc_ref[g] + _dot(vt_ref[0, g, j], p)

    scores_into(0, 0)

    def pair_step(p, carry):
        j = 2 * p
        scores_into(j + 1, 1)
        consume(j, 0, False)
        scores_into(j + 2, 0)
        consume(j + 1, 1, False)
        return carry

    lax.fori_loop(0, qi // 2, pair_step, 0)

    @pl.when(qi % 2 == 0)
    def _():
        consume(qi, 0, True)

    @pl.when(qi % 2 == 1)
    def _():
        scores_into(qi, 1)
        consume(qi - 1, 0, False)
        consume(qi, 1, True)
    for g, sl in enumerate(slabs):
        o_ref[0, :, sl] = (acc_ref[g, :V_HEAD, :] / acc_ref[g, V_HEAD:V_HEAD + 1, :]).T.astype(o_ref.dtype)


def _attn_b(q_nope, q_pe, k_nope, k_pe, v_t):
    b, s, _ = q_nope.shape
    t = v_t.shape[-1]
    width = B_GROUP * LANE
    qblk = (1, t, width)
    return pl.pallas_call(
        _attn_b_body,
        grid=(b, B_HEADS // B_GROUP, s // t),
        in_specs=[
            pl.BlockSpec(qblk, lambda bi, hg, i: (bi, i, hg)),
            pl.BlockSpec(qblk, lambda bi, hg, i: (bi, i, hg)),
            pl.BlockSpec((1, s, width), lambda bi, hg, i: (bi, 0, hg)),
            pl.BlockSpec((1, s, LANE), lambda bi, hg, i: (bi, 0, 0)),
            pl.BlockSpec((1, B_GROUP, s // t, B_VT_ROWS, t), lambda bi, hg, i: (bi, hg, 0, 0, 0)),
        ],
        out_specs=pl.BlockSpec(qblk, lambda bi, hg, i: (bi, i, hg)),
        out_shape=jax.ShapeDtypeStruct((b, s, B_WIDTH), BF16),
        scratch_shapes=[pltpu.VMEM((B_GROUP, 1, t), F32), pltpu.VMEM((B_GROUP, B_VT_ROWS, t), F32),
                        pltpu.VMEM((2, B_GROUP, t, t), F32)],
        compiler_params=_cparams(("parallel", "parallel", "arbitrary")),
        name="attn_b",
    )(q_nope, q_pe, k_nope, k_pe, v_t)


def _gate_mix_body(oa_ref, ob_ref, ga_ref, gb_ref, wa_ref, wb_ref, z_ref):
    ya = _dot(oa_ref[...], wa_ref[...])
    yb = _dot(ob_ref[...], wb_ref[...])
    z_ref[...] = (ga_ref[...].astype(F32) * ya + gb_ref[...].astype(F32) * yb).astype(z_ref.dtype)


def _gate_mix(o_a, o_b, gates, w_o_a, w_o_b):
    m = o_a.shape[0]
    tm = min(MM_ROW_TILE, m)
    tn = 1024
    nj = D_MODEL // tn
    return pl.pallas_call(
        _gate_mix_body,
        grid=(m // tm, nj),
        in_specs=[
            pl.BlockSpec((tm, A_WIDTH), lambda i, j: (i, 0)),
            pl.BlockSpec((tm, B_WIDTH), lambda i, j: (i, 0)),
            pl.BlockSpec((tm, tn), lambda i, j: (i, j)),
            pl.BlockSpec((tm, tn), lambda i, j: (i, nj + j)),
            pl.BlockSpec((A_WIDTH, tn), lambda i, j: (0, j)),
            pl.BlockSpec((B_WIDTH, tn), lambda i, j: (0, j)),
        ],
        out_specs=pl.BlockSpec((tm, tn), lambda i, j: (i, j)),
        out_shape=jax.ShapeDtypeStruct((m, D_MODEL), BF16),
        compiler_params=_cparams(("parallel", "parallel")),
        name="gate_mix",
    )(o_a, o_b, gates, gates, w_o_a.astype(BF16), w_o_b.astype(BF16))


def _proj_res_ln_body(x_ref, w_ref, res_ref, g_ref, b_ref, o_ref):
    y = ALPHA * res_ref[...] + _dot(x_ref[...], w_ref[...])
    o_ref[...] = _layer_norm(y, g_ref[...], b_ref[...])


def _proj_res_ln(x, w, res, ln_g, ln_b):
    m, k = x.shape
    tm = min(ROW_TILE, m)
    row = lambda i: (i, 0)
    const = lambda i: (0, 0)
    return pl.pallas_call(
        _proj_res_ln_body,
        grid=(m // tm,),
        in_specs=[
            pl.BlockSpec((tm, k), row),
            pl.BlockSpec((k, D_MODEL), const),
            pl.BlockSpec((tm, D_MODEL), row),
            pl.BlockSpec((1, D_MODEL), const),
            pl.BlockSpec((1, D_MODEL), const),
        ],
        out_specs=pl.BlockSpec((tm, D_MODEL), row),
        out_shape=jax.ShapeDtypeStruct((m, D_MODEL), F32),
        compiler_params=_cparams(("parallel",)),
        name="proj_res_ln",
    )(x, w.astype(BF16), res, ln_g.reshape(1, -1), ln_b.reshape(1, -1))


def _xattn_body(q_ref, kv_ref, w_ref, res_ref, g_ref, b_ref, o_ref, att_ref):
    for h in range(X_HEADS):
        sl = slice(h * X_HEAD_DIM, (h + 1) * X_HEAD_DIM)
        vs = slice(D_MODEL + h * X_HEAD_DIM, D_MODEL + (h + 1) * X_HEAD_DIM)
        s = _dot_nt(q_ref[0, :, sl], kv_ref[0, :, sl])
        mx = jnp.max(s, axis=-1, keepdims=True)
        e = jnp.exp(s - mx)
        den = jnp.sum(e, axis=-1, keepdims=True)
        att_ref[:, sl] = (_dot(e.astype(BF16), kv_ref[0, :, vs]) / den).astype(BF16)
    y = ALPHA * res_ref[0] + _dot(att_ref[...], w_ref[...])
    o_ref[0] = _layer_norm(y, g_ref[...], b_ref[...])


def _xattn(q, kv, w_xo, res, ln_g, ln_b):
    b, s, _ = q.shape
    tm = min(ROW_TILE, s)
    m_len = kv.shape[1]
    row = lambda bi, i: (bi, i, 0)
    return pl.pallas_call(
        _xattn_body,
        grid=(b, s // tm),
        in_specs=[
            pl.BlockSpec((1, tm, D_MODEL), row),
            pl.BlockSpec((1, m_len, 2 * D_MODEL), lambda bi, i: (bi, 0, 0)),
            pl.BlockSpec((D_MODEL, D_MODEL), lambda bi, i: (0, 0)),
            pl.BlockSpec((1, tm, D_MODEL), row),
            pl.BlockSpec((1, D_MODEL), lambda bi, i: (0, 0)),
            pl.BlockSpec((1, D_MODEL), lambda bi, i: (0, 0)),
        ],
        out_specs=pl.BlockSpec((1, tm, D_MODEL), row),
        out_shape=jax.ShapeDtypeStruct((b, s, D_MODEL), F32),
        scratch_shapes=[pltpu.VMEM((tm, D_MODEL), BF16)],
        compiler_params=_cparams(("parallel", "parallel")),
        name="xattn",
    )(q, kv, w_xo.astype(BF16), res, ln_g.reshape(1, -1), ln_b.reshape(1, -1))


def _in_proj_weights(w_in):
    qkv_end = 3 * A_WIDTH
    lat_end = qkv_end + Q_LORA + KV_LORA
    half = QK_ROPE // 2
    w_qkv = w_in[:, :qkv_end].astype(BF16)
    x1 = w_in[:, lat_end:lat_end + half]
    x2 = w_in[:, lat_end + half:lat_end + QK_ROPE]
    kpe_a, kpe_b = _rot_pair(x1, x2, (D_MODEL, LANE - QK_ROPE))
    w_lat = jnp.concatenate([w_in[:, qkv_end:lat_end], kpe_a, kpe_b], axis=1).astype(BF16)
    w_gate = w_in[:, lat_end + QK_ROPE:].astype(BF16)
    return w_qkv, w_lat, w_gate


def kernel(x, mem, ffn1_w_in, ffn1_w_out, ln_ffn1_g, ln_ffn1_b, w_in, gate_bias, rel_bias, q_a_norm, w_q_b,
           kv_a_norm, w_kv_b, w_o_a, w_o_b, w_out, ln_mix_g, ln_mix_b, mem_ln_g, mem_ln_b, w_xq, w_xkv, w_xo,
           ln_x_g, ln_x_b, ffn2_w_in, ffn2_w_out, ln_ffn2_g, ln_ffn2_b):
    b, s, d = x.shape
    m = b * s
    h = x.reshape(m, d)
    for l in range(DEPTH):
        h = _ffn_ln(h, ffn1_w_in[l], ffn1_w_out[l], ln_ffn1_g[l], ln_ffn1_b[l])

        w_qkv, w_lat, w_gate = _in_proj_weights(w_in[l])
        qkv_scale = jnp.concatenate([jnp.full((A_WIDTH,), A_HEAD_DIM ** -0.5 * LOG2_E, F32),
                                     jnp.ones((2 * A_WIDTH,), F32)])
        qkv = _matmul(h, w_qkv, out_dtype=BF16, tn=1024, col_scale=qkv_scale, name="in_proj_qkv")
        gates = _matmul(h, w_gate, out_dtype=BF16, tn=1024, bias=gate_bias[l], act="sigmoid", name="in_proj_gates")
        lat = _matmul(h, w_lat, out_dtype=F32, tn=LAT_COLS, name="in_proj_latent")
        o_a = _attn_a(qkv.reshape(b, s, 3 * A_WIDTH), rel_bias[l])
        q_nope, q_pe, k_nope, v_t, k_pe = _latent_up(lat, q_a_norm[l], w_q_b[l], kv_a_norm[l], w_kv_b[l], s)
        o_b = _attn_b(q_nope.reshape(b, s, B_WIDTH), q_pe.reshape(b, s, B_WIDTH),
                      k_nope.reshape(b, s, B_WIDTH), k_pe.reshape(b, s, LANE), v_t)
        z = _gate_mix(o_a.reshape(m, A_WIDTH), o_b.reshape(m, B_WIDTH), gates, w_o_a[l], w_o_b[l])
        h = _proj_res_ln(z, w_out[l], h, ln_mix_g[l], ln_mix_b[l])

        m_len = mem.shape[1]
        kv_x = _matmul(mem.reshape(b * m_len, d), w_xkv[l].astype(BF16), out_dtype=BF16, tn=1024,
                       pro="ln", pro_g=mem_ln_g[l], pro_b=mem_ln_b[l], name="mem_kv")
        q_x = _matmul(h, w_xq[l].astype(BF16), out_dtype=BF16, tn=1024, out_scale=X_HEAD_DIM ** -0.5, name="xattn_q")
        h = _xattn(q_x.reshape(b, s, d), kv_x.reshape(b, m_len, 2 * d), w_xo[l], h.reshape(b, s, d),
                   ln_x_g[l], ln_x_b[l]).reshape(m, d)

        h = _ffn_ln(h, ffn2_w_in[l], ffn2_w_out[l], ln_ffn2_g[l], ln_ffn2_b[l])
    return h.reshape(b, s, d)
```

```python
import functools
import math

import jax
import jax.numpy as jnp
from jax import lax
from jax.experimental import pallas as pl
from jax.experimental.pallas import tpu as pltpu

BF16 = jnp.bfloat16
F32 = jnp.float32

D_MODEL = 2048
CHUNK = 64
LEFT_CHUNKS = 8
MEM_LEN = 256
A_HEADS = 16
A_HEAD_DIM = 64
A_WIDTH = A_HEADS * A_HEAD_DIM
MAX_REL = 128
B_HEADS = 16
QK_NOPE = 128
QK_ROPE = 64
V_HEAD = 128
Q_LORA = 512
KV_LORA = 512
ROPE_THETA = 10000.0
B_WIDTH = B_HEADS * V_HEAD
X_HEADS = 4
X_HEAD_DIM = D_MODEL // X_HEADS
D_FF = 5504
DEPTH = 1
ALPHA = (2.0 * DEPTH) ** 0.25
LN_EPS = 1e-5
RMS_EPS = 1e-6
NEG_INF = -1e30
LOG2_E = math.log2(math.e)

LANE = 128
VMEM_LIMIT_BYTES = 56 * 1024 * 1024

FF_TILE = 512
D_FF_PAD = ((D_FF + FF_TILE - 1) // FF_TILE) * FF_TILE
ROW_TILE = 512
FFN_ROW_TILE = 512
MM_ROW_TILE = 1024
A_Q_TILE = 256
A_LEFT = LEFT_CHUNKS * CHUNK
A_WIN = A_LEFT + A_Q_TILE
A_TOEP = A_Q_TILE + A_WIN
B_TILE = 512
B_GROUP = 4
B_VT_ROWS = V_HEAD + 16
LAT_COLS = Q_LORA + KV_LORA + 2 * LANE


def _cparams(semantics):
    return pltpu.CompilerParams(dimension_semantics=semantics, vmem_limit_bytes=VMEM_LIMIT_BYTES)


def _layer_norm(x, g, b):
    mu = jnp.mean(x, axis=-1, keepdims=True)
    xc = x - mu
    var = jnp.mean(xc * xc, axis=-1, keepdims=True)
    return xc * lax.rsqrt(var + LN_EPS) * g + b


def _rms_norm(x, g):
    return x * lax.rsqrt(jnp.mean(x * x, axis=-1, keepdims=True) + RMS_EPS) * g


def _dot(a, b):
    return jnp.dot(a, b, preferred_element_type=F32)


def _dot_nt(a, b):
    return lax.dot_general(a, b, (((1,), (1,)), ((), ())), preferred_element_type=F32)


def _ffn_body(x_ref, wg_ref, *rest):
    up_refs = rest[:FF_TILE // LANE]
    wo_ref, g_ref, b_ref, o_ref, xb_ref = rest[FF_TILE // LANE:]
    j = pl.program_id(1)

    @pl.when(j == 0)
    def _():
        xb_ref[...] = x_ref[...].astype(BF16)
        o_ref[...] = jnp.zeros_like(o_ref)

    xb = xb_ref[...]
    gate = _dot(xb, wg_ref[...])
    up = _dot(xb, jnp.concatenate([r[...] for r in up_refs], axis=1))
    act = (gate * jax.nn.sigmoid(gate)) * up
    o_ref[...] += _dot(act.astype(BF16), wo_ref[...])

    @pl.when(j == pl.num_programs(1) - 1)
    def _():
        y = ALPHA * x_ref[...] + 0.5 * o_ref[...]
        o_ref[...] = _layer_norm(y, g_ref[...], b_ref[...])


def _ffn_ln(x, w_in, w_out, ln_g, ln_b):
    m = x.shape[0]
    tm = min(FFN_ROW_TILE, m)
    w_gu = w_in.astype(BF16)
    wo = jnp.concatenate([w_out.astype(BF16), jnp.zeros((D_FF_PAD - D_FF, D_MODEL), BF16)], axis=0)
    slabs = FF_TILE // LANE
    up_first = D_FF // LANE
    last_slab = 2 * D_FF // LANE - 1

    def up_spec(r):
        return pl.BlockSpec((D_MODEL, LANE), lambda i, j: (0, jnp.minimum(up_first + slabs * j + r, last_slab)))

    return pl.pallas_call(
        _ffn_body,
        grid=(m // tm, D_FF_PAD // FF_TILE),
        in_specs=[
            pl.BlockSpec((tm, D_MODEL), lambda i, j: (i, 0)),
            pl.BlockSpec((D_MODEL, FF_TILE), lambda i, j: (0, j)),
        ] + [up_spec(r) for r in range(slabs)] + [
            pl.BlockSpec((FF_TILE, D_MODEL), lambda i, j: (j, 0)),
            pl.BlockSpec((1, D_MODEL), lambda i, j: (0, 0)),
            pl.BlockSpec((1, D_MODEL), lambda i, j: (0, 0)),
        ],
        out_specs=pl.BlockSpec((tm, D_MODEL), lambda i, j: (i, 0)),
        out_shape=jax.ShapeDtypeStruct((m, D_MODEL), F32),
        scratch_shapes=[pltpu.VMEM((tm, D_MODEL), BF16)],
        compiler_params=_cparams(("parallel", "arbitrary")),
        name="ffn_ln",
    )(x, w_gu, *([w_gu] * slabs), wo, ln_g.reshape(1, -1), ln_b.reshape(1, -1))


def _mm_body(*refs, ln_in, has_bias, act, out_scale, has_col_scale):
    it = iter(refs)
    x_ref = next(it)
    pg_ref = next(it) if ln_in else None
    pb_ref = next(it) if ln_in else None
    w_ref = next(it)
    bias_ref = next(it) if has_bias else None
    cs_ref = next(it) if has_col_scale else None
    o_ref = next(it)
    x = x_ref[...]
    if ln_in:
        x = _layer_norm(x.astype(F32), pg_ref[...], pb_ref[...])
    y = _dot(x.astype(BF16), w_ref[...])
    if has_bias:
        y = y + bias_ref[...]
    if act == "sigmoid":
        y = jax.nn.sigmoid(y)
    if out_scale is not None:
        y = y * out_scale
    if has_col_scale:
        y = y * cs_ref[...]
    o_ref[...] = y.astype(o_ref.dtype)


def _matmul(x, w, *, out_dtype, tn, ln_g=None, ln_b=None,
            bias=None, act=None, out_scale=None, col_scale=None, name="matmul"):
    m = x.shape[0]
    k, n = w.shape
    tm = min(MM_ROW_TILE, m)
    args = [x]
    specs = [pl.BlockSpec((tm, k), lambda i, j: (i, 0))]
    if ln_g is not None:
        args += [ln_g.reshape(1, k), ln_b.reshape(1, k)]
        specs += [pl.BlockSpec((1, k), lambda i, j: (0, 0))] * 2
    args.append(w)
    specs.append(pl.BlockSpec((k, tn), lambda i, j: (0, j)))
    if bias is not None:
        args.append(bias.reshape(1, n))
        specs.append(pl.BlockSpec((1, tn), lambda i, j: (0, j)))
    if col_scale is not None:
        args.append(col_scale.reshape(1, n))
        specs.append(pl.BlockSpec((1, tn), lambda i, j: (0, j)))
    body = functools.partial(_mm_body, ln_in=ln_g is not None, has_bias=bias is not None, act=act, out_scale=out_scale,
                             has_col_scale=col_scale is not None)
    return pl.pallas_call(
        body,
        grid=(m // tm, n // tn),
        in_specs=specs,
        out_specs=pl.BlockSpec((tm, tn), lambda i, j: (i, j)),
        out_shape=jax.ShapeDtypeStruct((m, n), out_dtype),
        compiler_params=_cparams(("parallel", "parallel")),
        name=name,
    )(*args)


def _q_up_body(lat_ref, g_ref, w_ref, cos_ref, sin_ref, qn_ref, qp_ref, *, scale):
    c = _rms_norm(lat_ref[...], g_ref[...]).astype(BF16)
    qn_ref[...] = (_dot(c, w_ref[:, :B_WIDTH]) * scale).astype(BF16)
    pa = _dot(c, w_ref[:, B_WIDTH:2 * B_WIDTH])
    pb = _dot(c, w_ref[:, 2 * B_WIDTH:])
    cos = cos_ref[...] * scale
    sin = sin_ref[...] * scale
    for h in range(B_HEADS):
        sl = slice(h * LANE, (h + 1) * LANE)
        qp_ref[:, sl] = (pa[:, sl] * cos + pb[:, sl] * sin).astype(BF16)


def _kv_up_body(c_ref, g_ref, wk_ref, wvt_ref, ka_ref, kb_ref, cos_ref, sin_ref, kn_ref, vt_ref, kpe_ref):
    c = _rms_norm(c_ref[...], g_ref[...]).astype(BF16)
    n_chunk = 1024
    for n in range(B_WIDTH // n_chunk):
        sl = slice(n * n_chunk, (n + 1) * n_chunk)
        kn_ref[:, sl] = _dot(c, wk_ref[:, sl]).astype(BF16)
    heads_per_dot = 4
    tm = c.shape[0]
    ones_rows = (lax.broadcasted_iota(jnp.int32, (B_VT_ROWS - V_HEAD, tm), 0) == 0).astype(BF16)
    for hc in range(B_HEADS // heads_per_dot):
        rows = slice(hc * heads_per_dot * V_HEAD, (hc + 1) * heads_per_dot * V_HEAD)
        vt = _dot_nt(wvt_ref[rows, :], c).astype(BF16)
        for r in range(heads_per_dot):
            vt_ref[0, hc * heads_per_dot + r, 0, :V_HEAD, :] = vt[r * V_HEAD:(r + 1) * V_HEAD, :]
            vt_ref[0, hc * heads_per_dot + r, 0, V_HEAD:, :] = ones_rows
    kpe_ref[...] = (ka_ref[...] * cos_ref[...] + kb_ref[...] * sin_ref[...]).astype(BF16)


def _rope_tables(seq_len):
    half = QK_ROPE // 2
    inv = 1.0 / (ROPE_THETA ** (jnp.arange(0, QK_ROPE, 2, dtype=F32) / QK_ROPE))
    ang = jnp.arange(seq_len, dtype=F32)[:, None] * inv[None, :]
    zeros = jnp.zeros((seq_len, LANE - 2 * half), F32)
    cos = jnp.concatenate([jnp.cos(ang), jnp.cos(ang), zeros], axis=1)
    sin = jnp.concatenate([jnp.sin(ang), jnp.sin(ang), zeros], axis=1)
    return cos, sin


def _rot_pair(x1, x2, pad_shape):
    zeros = jnp.zeros(pad_shape, x1.dtype)
    return (jnp.concatenate([x1, x2, zeros], axis=-1), jnp.concatenate([-x2, x1, zeros], axis=-1))


def _latent_up(lat, q_a_norm, w_q_b, kv_a_norm, w_kv_b, seq_len):
    m = lat.shape[0]
    tm = min(B_TILE, seq_len)
    n_pos = seq_len // tm
    cos, sin = _rope_tables(seq_len)
    half = QK_ROPE // 2
    wq = w_q_b.reshape(Q_LORA, B_HEADS, QK_NOPE + QK_ROPE)
    pe = wq[:, :, QK_NOPE:]
    pe_a, pe_b = _rot_pair(pe[..., :half], pe[..., half:], (Q_LORA, B_HEADS, LANE - QK_ROPE))
    w_q_all = jnp.concatenate([wq[:, :, :QK_NOPE].reshape(Q_LORA, B_WIDTH),
                               pe_a.reshape(Q_LORA, B_WIDTH), pe_b.reshape(Q_LORA, B_WIDTH)],
                              axis=1).astype(BF16)
    wkv = w_kv_b.reshape(KV_LORA, B_HEADS, QK_NOPE + V_HEAD)
    w_k = wkv[:, :, :QK_NOPE].reshape(KV_LORA, B_WIDTH).astype(BF16)
    w_vt = wkv[:, :, QK_NOPE:].reshape(KV_LORA, B_WIDTH).T.astype(BF16)
    scale = (QK_NOPE + QK_ROPE) ** -0.5 * math.log2(math.e)
    row = lambda i: (i, 0)
    const = lambda i: (0, 0)
    pos = lambda i: (i % n_pos, 0)
    q_nope, q_pe = pl.pallas_call(
        functools.partial(_q_up_body, scale=scale),
        grid=(m // tm,),
        in_specs=[
            pl.BlockSpec((tm, Q_LORA), row),
            pl.BlockSpec((1, Q_LORA), const),
            pl.BlockSpec((Q_LORA, 3 * B_WIDTH), const),
            pl.BlockSpec((tm, LANE), pos),
            pl.BlockSpec((tm, LANE), pos),
        ],
        out_specs=[pl.BlockSpec((tm, B_WIDTH), row), pl.BlockSpec((tm, B_WIDTH), row)],
        out_shape=[jax.ShapeDtypeStruct((m, B_WIDTH), BF16), jax.ShapeDtypeStruct((m, B_WIDTH), BF16)],
        compiler_params=_cparams(("parallel",)),
        name="q_up",
    )(lat, q_a_norm.reshape(1, -1), w_q_all, cos, sin)
    ka_blk = (Q_LORA + KV_LORA) // LANE
    batch = m // seq_len
    k_nope, v_t, k_pe = pl.pallas_call(
        _kv_up_body,
        grid=(m // tm,),
        in_specs=[
            pl.BlockSpec((tm, KV_LORA), lambda i: (i, Q_LORA // KV_LORA)),
            pl.BlockSpec((1, KV_LORA), const),
            pl.BlockSpec((KV_LORA, B_WIDTH), const),
            pl.BlockSpec((B_WIDTH, KV_LORA), const),
            pl.BlockSpec((tm, LANE), lambda i: (i, ka_blk)),
            pl.BlockSpec((tm, LANE), lambda i: (i, ka_blk + 1)),
            pl.BlockSpec((tm, LANE), pos),
            pl.BlockSpec((tm, LANE), pos),
        ],
        out_specs=[pl.BlockSpec((tm, B_WIDTH), row),
                   pl.BlockSpec((1, B_HEADS, 1, B_VT_ROWS, tm), lambda i: (i // n_pos, 0, i % n_pos, 0, 0)),
                   pl.BlockSpec((tm, LANE), row)],
        out_shape=[jax.ShapeDtypeStruct((m, B_WIDTH), BF16),
                   jax.ShapeDtypeStruct((batch, B_HEADS, n_pos, B_VT_ROWS, tm), BF16),
                   jax.ShapeDtypeStruct((m, LANE), BF16)],
        compiler_params=_cparams(("parallel",)),
        name="kv_up",
    )(lat, kv_a_norm.reshape(1, -1), w_k, w_vt, lat, lat, cos, sin)
    return q_nope, q_pe, k_nope, v_t, k_pe


def _attn_a_body(q_ref, k0_ref, k1_ref, k2_ref, v0_ref, v1_ref, v2_ref, toep_ref, o_ref, bias_ref):
    i = pl.program_id(1)

    @pl.when((pl.program_id(0) == 0) & (i == 0))
    def _():
        shift = int(math.log2(CHUNK))
        qc = lax.broadcasted_iota(jnp.int32, (A_Q_TILE, A_WIN), 0) >> shift
        kc = lax.broadcasted_iota(jnp.int32, (A_Q_TILE, A_WIN), 1) >> shift
        band = (kc >= qc) & (kc <= qc + LEFT_CHUNKS)
        for h in range(A_HEADS):
            rows = jnp.broadcast_to(toep_ref[h:h + 1, :], (A_Q_TILE, A_TOEP))
            tile = pltpu.roll(rows, 0, 1, stride=1, stride_axis=0)
            bias_ref[h] = jnp.where(band, tile[:, :A_WIN] * LOG2_E, NEG_INF)

    col = lax.broadcasted_iota(jnp.int32, (A_Q_TILE, A_WIN), 1)
    in_seq = col >= (A_LEFT // A_Q_TILE - i) * A_Q_TILE
    lane = lax.broadcasted_iota(jnp.int32, (1, LANE), 1)
    low = lane < A_HEAD_DIM
    ones = jnp.ones((A_WIN, LANE), BF16)
    for hp in range(A_WIDTH // LANE):
        sl = slice(hp * LANE, (hp + 1) * LANE)
        qp = q_ref[0, :, sl]
        kp = jnp.concatenate([k0_ref[0, :, sl], k1_ref[0, :, sl], k2_ref[0, :, sl]], axis=0)
        vp = jnp.concatenate([v0_ref[0, :, sl], v1_ref[0, :, sl], v2_ref[0, :, sl]], axis=0)
        vp_ones = jnp.concatenate([vp, ones], axis=1)
        outs = []
        for sub in range(2):
            sel = low if sub == 0 else jnp.logical_not(low)
            qm = jnp.where(sel, qp, jnp.zeros_like(qp))
            s = _dot_nt(qm, kp)
            s = jnp.where(in_seq, s + bias_ref[2 * hp + sub], NEG_INF)
            mx = jnp.max(s, axis=-1, keepdims=True)
            e = jnp.exp2(s - mx).astype(BF16)
            o_den = _dot(e, vp_ones)
            outs.append(o_den[:, :LANE] / o_den[:, LANE:])
        o_ref[0, :, sl] = jnp.where(low, outs[0], outs[1]).astype(o_ref.dtype)


def _attn_a_toeplitz_row(rel_bias):
    n_head = rel_bias.shape[0]
    far_left = jnp.broadcast_to(rel_bias[:, -1:], (n_head, A_LEFT - MAX_REL))
    far_right = jnp.broadcast_to(rel_bias[:, :1], (n_head, A_WIN - A_LEFT - MAX_REL))
    wrapped = jnp.broadcast_to(rel_bias[:, -1:], (n_head, A_Q_TILE - 1))
    return jnp.concatenate([far_left, rel_bias[:, ::-1], far_right, wrapped], axis=1).astype(F32)


def _attn_a(qkv, rel_bias):
    b, s, _ = qkv.shape
    nq = s // A_Q_TILE
    back = A_LEFT // A_Q_TILE
    blk = (1, A_Q_TILE, A_WIDTH)

    def kv_spec(col, off):
        return pl.BlockSpec(blk, lambda bi, i: (bi, jnp.maximum(i - back + off, 0), col))

    return pl.pallas_call(
        _attn_a_body,
        grid=(b, nq),
        in_specs=[pl.BlockSpec(blk, lambda bi, i: (bi, i, 0))]
        + [kv_spec(1, off) for off in range(back + 1)]
        + [kv_spec(2, off) for off in range(back + 1)]
        + [pl.BlockSpec((A_HEADS, A_TOEP), lambda bi, i: (0, 0))],
        out_specs=pl.BlockSpec(blk, lambda bi, i: (bi, i, 0)),
        out_shape=jax.ShapeDtypeStruct((b, s, A_WIDTH), BF16),
        scratch_shapes=[pltpu.VMEM((A_HEADS, A_Q_TILE, A_WIN), F32)],
        compiler_params=_cparams(("arbitrary", "arbitrary")),
        name="attn_a",
    )(qkv, qkv, qkv, qkv, qkv, qkv, qkv, _attn_a_toeplitz_row(rel_bias))


def _attn_b_body(qn_ref, qp_ref, kn_ref, kpe_ref, vt_ref, o_ref, m_ref, acc_ref, s_ref):
    qi = pl.program_id(2)
    t = qn_ref.shape[1]
    slabs = [slice(g * LANE, (g + 1) * LANE) for g in range(B_GROUP)]
    qs = [jnp.concatenate([qn_ref[0, :, sl], qp_ref[0, :, sl]], axis=-1) for sl in slabs]
    shift = int(math.log2(CHUNK))
    key_chunk = lax.broadcasted_iota(jnp.int32, (t, t), 0) >> shift
    qry_chunk = lax.broadcasted_iota(jnp.int32, (t, t), 1) >> shift
    diag_allowed = key_chunk <= qry_chunk

    m_ref[...] = jnp.full_like(m_ref, NEG_INF)
    acc_ref[...] = jnp.zeros_like(acc_ref)

    def scores_into(j, slot):
        rows = pl.ds(pl.multiple_of(j * t, t), t)
        kpe = kpe_ref[0, rows, :]
        for g, sl in enumerate(slabs):
            s_ref[slot, g] = _dot_nt(jnp.concatenate([kn_ref[0, rows, sl], kpe], axis=-1), qs[g])

    def consume(j, slot, masked):
        for g in range(B_GROUP):
            st = s_ref[slot, g]
            if masked:
                st = jnp.where(diag_allowed, st, NEG_INF)
            m_old = m_ref[g]
            m_new = jnp.maximum(m_old, jnp.max(st, axis=0, keepdims=True))
            corr = jnp.exp2(m_old - m_new)
            p = jnp.exp2(st - m_new).astype(BF16)
            m_ref[g] = m_new
            acc_ref[g] = corr * acc_ref[g] + _dot(vt_ref[0, g, j], p)

    scores_into(0, 0)

    def pair_step(p, carry):
        j = 2 * p
        scores_into(j + 1, 1)
        consume(j, 0, False)
        scores_into(j + 2, 0)
        consume(j + 1, 1, False)
        return carry

    lax.fori_loop(0, qi // 2, pair_step, 0)

    @pl.when(qi % 2 == 0)
    def _():
        consume(qi, 0, True)

    @pl.when(qi % 2 == 1)
    def _():
        scores_into(qi, 1)
        consume(qi - 1, 0, False)
        consume(qi, 1, True)
    for g, sl in enumerate(slabs):
        o_ref[0, :, sl] = (acc_ref[g, :V_HEAD, :] / acc_ref[g, V_HEAD:V_HEAD + 1, :]).T.astype(o_ref.dtype)


def _attn_b(q_nope, q_pe, k_nope, k_pe, v_t):
    b, s, _ = q_nope.shape
    t = v_t.shape[-1]
    width = B_GROUP * LANE
    qblk = (1, t, width)
    return pl.pallas_call(
        _attn_b_body,
        grid=(b, B_HEADS // B_GROUP, s // t),
        in_specs=[
            pl.BlockSpec(qblk, lambda bi, hg, i: (bi, i, hg)),
            pl.BlockSpec(qblk, lambda bi, hg, i: (bi, i, hg)),
            pl.BlockSpec((1, s, width), lambda bi, hg, i: (bi, 0, hg)),
            pl.BlockSpec((1, s, LANE), lambda bi, hg, i: (bi, 0, 0)),
            pl.BlockSpec((1, B_GROUP, s // t, B_VT_ROWS, t), lambda bi, hg, i: (bi, hg, 0, 0, 0)),
        ],
        out_specs=pl.BlockSpec(qblk, lambda bi, hg, i: (bi, i, hg)),
        out_shape=jax.ShapeDtypeStruct((b, s, B_WIDTH), BF16),
        scratch_shapes=[pltpu.VMEM((B_GROUP, 1, t), F32), pltpu.VMEM((B_GROUP, B_VT_ROWS, t), F32),
                        pltpu.VMEM((2, B_GROUP, t, t), F32)],
        compiler_params=_cparams(("parallel", "parallel", "arbitrary")),
        name="attn_b",
    )(q_nope, q_pe, k_nope, k_pe, v_t)


def _gate_mix_body(oa_ref, ob_ref, ga_ref, gb_ref, wa_ref, wb_ref, z_ref):
    ya = _dot(oa_ref[...], wa_ref[...])
    yb = _dot(ob_ref[...], wb_ref[...])
    z_ref[...] = (ga_ref[...].astype(F32) * ya + gb_ref[...].astype(F32) * yb).astype(z_ref.dtype)


def _gate_mix(o_a, o_b, gates, w_o_a, w_o_b):
    m = o_a.shape[0]
    tm = min(MM_ROW_TILE, m)
    tn = 1024
    nj = D_MODEL // tn
    return pl.pallas_call(
        _gate_mix_body,
        grid=(m // tm, nj),
        in_specs=[
            pl.BlockSpec((tm, A_WIDTH), lambda i, j: (i, 0)),
            pl.BlockSpec((tm, B_WIDTH), lambda i, j: (i, 0)),
            pl.BlockSpec((tm, tn), lambda i, j: (i, j)),
            pl.BlockSpec((tm, tn), lambda i, j: (i, nj + j)),
            pl.BlockSpec((A_WIDTH, tn), lambda i, j: (0, j)),
            pl.BlockSpec((B_WIDTH, tn), lambda i, j: (0, j)),
        ],
        out_specs=pl.BlockSpec((tm, tn), lambda i, j: (i, j)),
        out_shape=jax.ShapeDtypeStruct((m, D_MODEL), BF16),
        compiler_params=_cparams(("parallel", "parallel")),
        name="gate_mix",
    )(o_a, o_b, gates, gates, w_o_a.astype(BF16), w_o_b.astype(BF16))


def _proj_res_ln_body(x_ref, w_ref, res_ref, g_ref, b_ref, o_ref):
    y = ALPHA * res_ref[...] + _dot(x_ref[...], w_ref[...])
    o_ref[...] = _layer_norm(y, g_ref[...], b_ref[...])


def _proj_res_ln(x, w, res, ln_g, ln_b):
    m, k = x.shape
    tm = min(ROW_TILE, m)
    row = lambda i: (i, 0)
    const = lambda i: (0, 0)
    return pl.pallas_call(
        _proj_res_ln_body,
        grid=(m // tm,),
        in_specs=[
            pl.BlockSpec((tm, k), row),
            pl.BlockSpec((k, D_MODEL), const),
            pl.BlockSpec((tm, D_MODEL), row),
            pl.BlockSpec((1, D_MODEL), const),
            pl.BlockSpec((1, D_MODEL), const),
        ],
        out_specs=pl.BlockSpec((tm, D_MODEL), row),
        out_shape=jax.ShapeDtypeStruct((m, D_MODEL), F32),
        compiler_params=_cparams(("parallel",)),
        name="proj_res_ln",
    )(x, w.astype(BF16), res, ln_g.reshape(1, -1), ln_b.reshape(1, -1))


def _xattn_body(q_ref, kv_ref, w_ref, res_ref, g_ref, b_ref, o_ref, att_ref):
    for h in range(X_HEADS):
        sl = slice(h * X_HEAD_DIM, (h + 1) * X_HEAD_DIM)
        vs = slice(D_MODEL + h * X_HEAD_DIM, D_MODEL + (h + 1) * X_HEAD_DIM)
        s = _dot_nt(q_ref[0, :, sl], kv_ref[0, :, sl])
        mx = jnp.max(s, axis=-1, keepdims=True)
        e = jnp.exp(s - mx)
        den = jnp.sum(e, axis=-1, keepdims=True)
        att_ref[:, sl] = (_dot(e.astype(BF16), kv_ref[0, :, vs]) / den).astype(BF16)
    y = ALPHA * res_ref[0] + _dot(att_ref[...], w_ref[...])
    o_ref[0] = _layer_norm(y, g_ref[...], b_ref[...])


def _xattn(q, kv, w_xo, res, ln_g, ln_b):
    b, s, _ = q.shape
    tm = min(ROW_TILE, s)
    m_len = kv.shape[1]
    row = lambda bi, i: (bi, i, 0)
    return pl.pallas_call(
        _xattn_body,
        grid=(b, s // tm),
        in_specs=[
            pl.BlockSpec((1, tm, D_MODEL), row),
            pl.BlockSpec((1, m_len, 2 * D_MODEL), lambda bi, i: (bi, 0, 0)),
            pl.BlockSpec((D_MODEL, D_MODEL), lambda bi, i: (0, 0)),
            pl.BlockSpec((1, tm, D_MODEL), row),
            pl.BlockSpec((1, D_MODEL), lambda bi, i: (0, 0)),
            pl.BlockSpec((1, D_MODEL), lambda bi, i: (0, 0)),
        ],
        out_specs=pl.BlockSpec((1, tm, D_MODEL), row),
        out_shape=jax.ShapeDtypeStruct((b, s, D_MODEL), F32),
        scratch_shapes=[pltpu.VMEM((tm, D_MODEL), BF16)],
        compiler_params=_cparams(("parallel", "parallel")),
        name="xattn",
    )(q, kv, w_xo.astype(BF16), res, ln_g.reshape(1, -1), ln_b.reshape(1, -1))


def _in_proj_weights(w_in):
    qkv_end = 3 * A_WIDTH
    lat_end = qkv_end + Q_LORA + KV_LORA
    half = QK_ROPE // 2
    w_qkv = w_in[:, :qkv_end].astype(BF16)
    x1 = w_in[:, lat_end:lat_end + half]
    x2 = w_in[:, lat_end + half:lat_end + QK_ROPE]
    kpe_a, kpe_b = _rot_pair(x1, x2, (D_MODEL, LANE - QK_ROPE))
    w_lat = jnp.concatenate([w_in[:, qkv_end:lat_end], kpe_a, kpe_b], axis=1).astype(BF16)
    w_gate = w_in[:, lat_end + QK_ROPE:].astype(BF16)
    return w_qkv, w_lat, w_gate


def kernel(x, mem, ffn1_w_in, ffn1_w_out, ln_ffn1_g, ln_ffn1_b, w_in, gate_bias, rel_bias, q_a_norm, w_q_b,
           kv_a_norm, w_kv_b, w_o_a, w_o_b, w_out, ln_mix_g, ln_mix_b, mem_ln_g, mem_ln_b, w_xq, w_xkv, w_xo,
           ln_x_g, ln_x_b, ffn2_w_in, ffn2_w_out, ln_ffn2_g, ln_ffn2_b):
    b, s, d = x.shape
    m = b * s
    h = x.reshape(m, d)
    for l in range(DEPTH):
        h = _ffn_ln(h, ffn1_w_in[l], ffn1_w_out[l], ln_ffn1_g[l], ln_ffn1_b[l])

        w_qkv, w_lat, w_gate = _in_proj_weights(w_in[l])
        qkv_scale = jnp.concatenate([jnp.full((A_WIDTH,), A_HEAD_DIM ** -0.5 * LOG2_E, F32),
                                     jnp.ones((2 * A_WIDTH,), F32)])
        qkv = _matmul(h, w_qkv, out_dtype=BF16, tn=1024, col_scale=qkv_scale, name="in_proj_qkv")
        gates = _matmul(h, w_gate, out_dtype=BF16, tn=1024, bias=gate_bias[l], act="sigmoid", name="in_proj_gates")
        lat = _matmul(h, w_lat, out_dtype=F32, tn=LAT_COLS, name="in_proj_latent")
        o_a = _attn_a(qkv.reshape(b, s, 3 * A_WIDTH), rel_bias[l])
        q_nope, q_pe, k_nope, v_t, k_pe = _latent_up(lat, q_a_norm[l], w_q_b[l], kv_a_norm[l], w_kv_b[l], s)
        o_b = _attn_b(q_nope.reshape(b, s, B_WIDTH), q_pe.reshape(b, s, B_WIDTH),
                      k_nope.reshape(b, s, B_WIDTH), k_pe.reshape(b, s, LANE), v_t)
        z = _gate_mix(o_a.reshape(m, A_WIDTH), o_b.reshape(m, B_WIDTH), gates, w_o_a[l], w_o_b[l])
        h = _proj_res_ln(z, w_out[l], h, ln_mix_g[l], ln_mix_b[l])

        m_len = mem.shape[1]
        kv_x = _matmul(mem.reshape(b * m_len, d), w_xkv[l].astype(BF16), out_dtype=BF16, tn=1024,
                       ln_g=mem_ln_g[l], ln_b=mem_ln_b[l], name="mem_kv")
        q_x = _matmul(h, w_xq[l].astype(BF16), out_dtype=BF16, tn=1024, out_scale=X_HEAD_DIM ** -0.5, name="xattn_q")
        h = _xattn(q_x.reshape(b, s, d), kv_x.reshape(b, m_len, 2 * d), w_xo[l], h.reshape(b, s, d),
                   ln_x_g[l], ln_x_b[l]).reshape(m, d)

        h = _ffn_ln(h, ffn2_w_in[l], ffn2_w_out[l], ln_ffn2_g[l], ln_ffn2_b[l])
    return h.reshape(b, s, d)
```

```python
import functools
import math

import jax
import jax.numpy as jnp
from jax import lax
from jax.experimental import pallas as pl
from jax.experimental.pallas import tpu as pltpu

BF16 = jnp.bfloat16
F32 = jnp.float32

D_MODEL = 2048
CHUNK = 64
LEFT_CHUNKS = 8
MEM_LEN = 256
A_HEADS = 16
A_HEAD_DIM = 64
A_WIDTH = A_HEADS * A_HEAD_DIM
MAX_REL = 128
B_HEADS = 16
QK_NOPE = 128
QK_ROPE = 64
V_HEAD = 128
Q_LORA = 512
KV_LORA = 512
ROPE_THETA = 10000.0
B_WIDTH = B_HEADS * V_HEAD
X_HEADS = 4
X_HEAD_DIM = D_MODEL // X_HEADS
D_FF = 5504
DEPTH = 1
ALPHA = (2.0 * DEPTH) ** 0.25
LN_EPS = 1e-5
RMS_EPS = 1e-6
NEG_INF = -1e30
LOG2_E = math.log2(math.e)

LANE = 128
VMEM_LIMIT_BYTES = 56 * 1024 * 1024

FF_TILE = 256
D_FF_PAD = ((D_FF + FF_TILE - 1) // FF_TILE) * FF_TILE
ROW_TILE = 512
FFN_ROW_TILE = 1024
MM_ROW_TILE = 1024
A_Q_TILE = 256
A_LEFT = LEFT_CHUNKS * CHUNK
A_WIN = A_LEFT + A_Q_TILE
A_TOEP = A_Q_TILE + A_WIN
B_TILE = 512
B_GROUP = 4
B_VT_ROWS = V_HEAD + 16
LAT_COLS = Q_LORA + KV_LORA + 2 * LANE


def _cparams(semantics):
    return pltpu.CompilerParams(dimension_semantics=semantics, vmem_limit_bytes=VMEM_LIMIT_BYTES)


def _layer_norm(x, g, b):
    mu = jnp.mean(x, axis=-1, keepdims=True)
    xc = x - mu
    var = jnp.mean(xc * xc, axis=-1, keepdims=True)
    return xc * lax.rsqrt(var + LN_EPS) * g + b


def _rms_norm(x, g):
    return x * lax.rsqrt(jnp.mean(x * x, axis=-1, keepdims=True) + RMS_EPS) * g


def _dot(a, b):
    return jnp.dot(a, b, preferred_element_type=F32)


def _dot_nt(a, b):
    return lax.dot_general(a, b, (((1,), (1,)), ((), ())), preferred_element_type=F32)


def _ffn_body(x_ref, wg_ref, *rest):
    up_refs = rest[:FF_TILE // LANE]
    wo_ref, g_ref, b_ref, o_ref, xb_ref = rest[FF_TILE // LANE:]
    j = pl.program_id(1)

    @pl.when(j == 0)
    def _():
        xb_ref[...] = x_ref[...].astype(BF16)
        o_ref[...] = jnp.zeros_like(o_ref)

    xb = xb_ref[...]
    gate = _dot(xb, wg_ref[...])
    up = _dot(xb, jnp.concatenate([r[...] for r in up_refs], axis=1))
    act = (gate * jax.nn.sigmoid(gate)) * up
    o_ref[...] += _dot(act.astype(BF16), wo_ref[...])

    @pl.when(j == pl.num_programs(1) - 1)
    def _():
        y = ALPHA * x_ref[...] + 0.5 * o_ref[...]
        o_ref[...] = _layer_norm(y, g_ref[...], b_ref[...])


def _ffn_ln(x, w_in, w_out, ln_g, ln_b):
    m = x.shape[0]
    tm = min(FFN_ROW_TILE, m)
    w_gu = w_in.astype(BF16)
    wo = jnp.concatenate([w_out.astype(BF16), jnp.zeros((D_FF_PAD - D_FF, D_MODEL), BF16)], axis=0)
    slabs = FF_TILE // LANE
    up_first = D_FF // LANE
    last_slab = 2 * D_FF // LANE - 1

    def up_spec(r):
        return pl.BlockSpec((D_MODEL, LANE), lambda i, j: (0, jnp.minimum(up_first + slabs * j + r, last_slab)))

    return pl.pallas_call(
        _ffn_body,
        grid=(m // tm, D_FF_PAD // FF_TILE),
        in_specs=[
            pl.BlockSpec((tm, D_MODEL), lambda i, j: (i, 0)),
            pl.BlockSpec((D_MODEL, FF_TILE), lambda i, j: (0, j)),
        ] + [up_spec(r) for r in range(slabs)] + [
            pl.BlockSpec((FF_TILE, D_MODEL), lambda i, j: (j, 0)),
            pl.BlockSpec((1, D_MODEL), lambda i, j: (0, 0)),
            pl.BlockSpec((1, D_MODEL), lambda i, j: (0, 0)),
        ],
        out_specs=pl.BlockSpec((tm, D_MODEL), lambda i, j: (i, 0)),
        out_shape=jax.ShapeDtypeStruct((m, D_MODEL), F32),
        scratch_shapes=[pltpu.VMEM((tm, D_MODEL), BF16)],
        compiler_params=_cparams(("parallel", "arbitrary")),
        name="ffn_ln",
    )(x, w_gu, *([w_gu] * slabs), wo, ln_g.reshape(1, -1), ln_b.reshape(1, -1))


def _mm_body(*refs, ln_in, has_bias, act, out_scale, has_col_scale):
    it = iter(refs)
    x_ref = next(it)
    pg_ref = next(it) if ln_in else None
    pb_ref = next(it) if ln_in else None
    w_ref = next(it)
    bias_ref = next(it) if has_bias else None
    cs_ref = next(it) if has_col_scale else None
    o_ref = next(it)
    x = x_ref[...]
    if ln_in:
        x = _layer_norm(x.astype(F32), pg_ref[...], pb_ref[...])
    y = _dot(x.astype(BF16), w_ref[...])
    if has_bias:
        y = y + bias_ref[...]
    if act == "sigmoid":
        y = jax.nn.sigmoid(y)
    if out_scale is not None:
        y = y * out_scale
    if has_col_scale:
        y = y * cs_ref[...]
    o_ref[...] = y.astype(o_ref.dtype)


def _matmul(x, w, *, out_dtype, tn, ln_g=None, ln_b=None,
            bias=None, act=None, out_scale=None, col_scale=None, name="matmul"):
    m = x.shape[0]
    k, n = w.shape
    tm = min(MM_ROW_TILE, m)
    args = [x]
    specs = [pl.BlockSpec((tm, k), lambda i, j: (i, 0))]
    if ln_g is not None:
        args += [ln_g.reshape(1, k), ln_b.reshape(1, k)]
        specs += [pl.BlockSpec((1, k), lambda i, j: (0, 0))] * 2
    args.append(w)
    specs.append(pl.BlockSpec((k, tn), lambda i, j: (0, j)))
    if bias is not None:
        args.append(bias.reshape(1, n))
        specs.append(pl.BlockSpec((1, tn), lambda i, j: (0, j)))
    if col_scale is not None:
        args.append(col_scale.reshape(1, n))
        specs.append(pl.BlockSpec((1, tn), lambda i, j: (0, j)))
    body = functools.partial(_mm_body, ln_in=ln_g is not None, has_bias=bias is not None, act=act, out_scale=out_scale,
                             has_col_scale=col_scale is not None)
    return pl.pallas_call(
        body,
        grid=(m // tm, n // tn),
        in_specs=specs,
        out_specs=pl.BlockSpec((tm, tn), lambda i, j: (i, j)),
        out_shape=jax.ShapeDtypeStruct((m, n), out_dtype),
        compiler_params=_cparams(("parallel", "parallel")),
        name=name,
    )(*args)


def _q_up_body(lat_ref, g_ref, w_ref, cos_ref, sin_ref, qn_ref, qp_ref, *, scale):
    c = _rms_norm(lat_ref[...], g_ref[...]).astype(BF16)
    qn_ref[...] = (_dot(c, w_ref[:, :B_WIDTH]) * scale).astype(BF16)
    pa = _dot(c, w_ref[:, B_WIDTH:2 * B_WIDTH])
    pb = _dot(c, w_ref[:, 2 * B_WIDTH:])
    cos = cos_ref[...] * scale
    sin = sin_ref[...] * scale
    for h in range(B_HEADS):
        sl = slice(h * LANE, (h + 1) * LANE)
        qp_ref[:, sl] = (pa[:, sl] * cos + pb[:, sl] * sin).astype(BF16)


def _kv_up_body(c_ref, g_ref, wk_ref, wvt_ref, ka_ref, kb_ref, cos_ref, sin_ref, kn_ref, vt_ref, kpe_ref):
    c = _rms_norm(c_ref[...], g_ref[...]).astype(BF16)
    n_chunk = 1024
    for n in range(B_WIDTH // n_chunk):
        sl = slice(n * n_chunk, (n + 1) * n_chunk)
        kn_ref[:, sl] = _dot(c, wk_ref[:, sl]).astype(BF16)
    heads_per_dot = 4
    tm = c.shape[0]
    ones_rows = (lax.broadcasted_iota(jnp.int32, (B_VT_ROWS - V_HEAD, tm), 0) == 0).astype(BF16)
    for hc in range(B_HEADS // heads_per_dot):
        rows = slice(hc * heads_per_dot * V_HEAD, (hc + 1) * heads_per_dot * V_HEAD)
        vt = _dot_nt(wvt_ref[rows, :], c).astype(BF16)
        for r in range(heads_per_dot):
            vt_ref[0, hc * heads_per_dot + r, 0, :V_HEAD, :] = vt[r * V_HEAD:(r + 1) * V_HEAD, :]
            vt_ref[0, hc * heads_per_dot + r, 0, V_HEAD:, :] = ones_rows
    kpe_ref[...] = (ka_ref[...] * cos_ref[...] + kb_ref[...] * sin_ref[...]).astype(BF16)


def _rope_tables(seq_len):
    half = QK_ROPE // 2
    inv = 1.0 / (ROPE_THETA ** (jnp.arange(0, QK_ROPE, 2, dtype=F32) / QK_ROPE))
    ang = jnp.arange(seq_len, dtype=F32)[:, None] * inv[None, :]
    zeros = jnp.zeros((seq_len, LANE - 2 * half), F32)
    cos = jnp.concatenate([jnp.cos(ang), jnp.cos(ang), zeros], axis=1)
    sin = jnp.concatenate([jnp.sin(ang), jnp.sin(ang), zeros], axis=1)
    return cos, sin


def _rot_pair(x1, x2, pad_shape):
    zeros = jnp.zeros(pad_shape, x1.dtype)
    return (jnp.concatenate([x1, x2, zeros], axis=-1), jnp.concatenate([-x2, x1, zeros], axis=-1))


def _latent_up(lat, q_a_norm, w_q_b, kv_a_norm, w_kv_b, seq_len):
    m = lat.shape[0]
    tm = min(B_TILE, seq_len)
    n_pos = seq_len // tm
    cos, sin = _rope_tables(seq_len)
    half = QK_ROPE // 2
    wq = w_q_b.reshape(Q_LORA, B_HEADS, QK_NOPE + QK_ROPE)
    pe = wq[:, :, QK_NOPE:]
    pe_a, pe_b = _rot_pair(pe[..., :half], pe[..., half:], (Q_LORA, B_HEADS, LANE - QK_ROPE))
    w_q_all = jnp.concatenate([wq[:, :, :QK_NOPE].reshape(Q_LORA, B_WIDTH),
                               pe_a.reshape(Q_LORA, B_WIDTH), pe_b.reshape(Q_LORA, B_WIDTH)],
                              axis=1).astype(BF16)
    wkv = w_kv_b.reshape(KV_LORA, B_HEADS, QK_NOPE + V_HEAD)
    w_k = wkv[:, :, :QK_NOPE].reshape(KV_LORA, B_WIDTH).astype(BF16)
    w_vt = wkv[:, :, QK_NOPE:].reshape(KV_LORA, B_WIDTH).T.astype(BF16)
    scale = (QK_NOPE + QK_ROPE) ** -0.5 * math.log2(math.e)
    row = lambda i: (i, 0)
    const = lambda i: (0, 0)
    pos = lambda i: (i % n_pos, 0)
    q_nope, q_pe = pl.pallas_call(
        functools.partial(_q_up_body, scale=scale),
        grid=(m // tm,),
        in_specs=[
            pl.BlockSpec((tm, Q_LORA), row),
            pl.BlockSpec((1, Q_LORA), const),
            pl.BlockSpec((Q_LORA, 3 * B_WIDTH), const),
            pl.BlockSpec((tm, LANE), pos),
            pl.BlockSpec((tm, LANE), pos),
        ],
        out_specs=[pl.BlockSpec((tm, B_WIDTH), row), pl.BlockSpec((tm, B_WIDTH), row)],
        out_shape=[jax.ShapeDtypeStruct((m, B_WIDTH), BF16), jax.ShapeDtypeStruct((m, B_WIDTH), BF16)],
        compiler_params=_cparams(("parallel",)),
        name="q_up",
    )(lat, q_a_norm.reshape(1, -1), w_q_all, cos, sin)
    ka_blk = (Q_LORA + KV_LORA) // LANE
    batch = m // seq_len
    k_nope, v_t, k_pe = pl.pallas_call(
        _kv_up_body,
        grid=(m // tm,),
        in_specs=[
            pl.BlockSpec((tm, KV_LORA), lambda i: (i, Q_LORA // KV_LORA)),
            pl.BlockSpec((1, KV_LORA), const),
            pl.BlockSpec((KV_LORA, B_WIDTH), const),
            pl.BlockSpec((B_WIDTH, KV_LORA), const),
            pl.BlockSpec((tm, LANE), lambda i: (i, ka_blk)),
            pl.BlockSpec((tm, LANE), lambda i: (i, ka_blk + 1)),
            pl.BlockSpec((tm, LANE), pos),
            pl.BlockSpec((tm, LANE), pos),
        ],
        out_specs=[pl.BlockSpec((tm, B_WIDTH), row),
                   pl.BlockSpec((1, B_HEADS, 1, B_VT_ROWS, tm), lambda i: (i // n_pos, 0, i % n_pos, 0, 0)),
                   pl.BlockSpec((tm, LANE), row)],
        out_shape=[jax.ShapeDtypeStruct((m, B_WIDTH), BF16),
                   jax.ShapeDtypeStruct((batch, B_HEADS, n_pos, B_VT_ROWS, tm), BF16),
                   jax.ShapeDtypeStruct((m, LANE), BF16)],
        compiler_params=_cparams(("parallel",)),
        name="kv_up",
    )(lat, kv_a_norm.reshape(1, -1), w_k, w_vt, lat, lat, cos, sin)
    return q_nope, q_pe, k_nope, v_t, k_pe


def _attn_a_body(q_ref, k0_ref, k1_ref, k2_ref, v0_ref, v1_ref, v2_ref, toep_ref, o_ref, bias_ref):
    i = pl.program_id(1)

    @pl.when((pl.program_id(0) == 0) & (i == 0))
    def _():
        shift = int(math.log2(CHUNK))
        qc = lax.broadcasted_iota(jnp.int32, (A_Q_TILE, A_WIN), 0) >> shift
        kc = lax.broadcasted_iota(jnp.int32, (A_Q_TILE, A_WIN), 1) >> shift
        band = (kc >= qc) & (kc <= qc + LEFT_CHUNKS)
        for h in range(A_HEADS):
            rows = jnp.broadcast_to(toep_ref[h:h + 1, :], (A_Q_TILE, A_TOEP))
            tile = pltpu.roll(rows, 0, 1, stride=1, stride_axis=0)
            bias_ref[h] = jnp.where(band, tile[:, :A_WIN] * LOG2_E, NEG_INF)

    col = lax.broadcasted_iota(jnp.int32, (A_Q_TILE, A_WIN), 1)
    in_seq = col >= (A_LEFT // A_Q_TILE - i) * A_Q_TILE
    lane = lax.broadcasted_iota(jnp.int32, (1, LANE), 1)
    low = lane < A_HEAD_DIM
    ones = jnp.ones((A_WIN, LANE), BF16)
    for hp in range(A_WIDTH // LANE):
        sl = slice(hp * LANE, (hp + 1) * LANE)
        qp = q_ref[0, :, sl]
        kp = jnp.concatenate([k0_ref[0, :, sl], k1_ref[0, :, sl], k2_ref[0, :, sl]], axis=0)
        vp = jnp.concatenate([v0_ref[0, :, sl], v1_ref[0, :, sl], v2_ref[0, :, sl]], axis=0)
        vp_ones = jnp.concatenate([vp, ones], axis=1)
        outs = []
        for sub in range(2):
            sel = low if sub == 0 else jnp.logical_not(low)
            qm = jnp.where(sel, qp, jnp.zeros_like(qp))
            s = _dot_nt(qm, kp)
            s = jnp.where(in_seq, s + bias_ref[2 * hp + sub], NEG_INF)
            mx = jnp.max(s, axis=-1, keepdims=True)
            e = jnp.exp2(s - mx).astype(BF16)
            o_den = _dot(e, vp_ones)
            outs.append(o_den[:, :LANE] / o_den[:, LANE:])
        o_ref[0, :, sl] = jnp.where(low, outs[0], outs[1]).astype(o_ref.dtype)


def _attn_a_toeplitz_row(rel_bias):
    n_head = rel_bias.shape[0]
    far_left = jnp.broadcast_to(rel_bias[:, -1:], (n_head, A_LEFT - MAX_REL))
    far_right = jnp.broadcast_to(rel_bias[:, :1], (n_head, A_WIN - A_LEFT - MAX_REL))
    wrapped = jnp.broadcast_to(rel_bias[:, -1:], (n_head, A_Q_TILE - 1))
    return jnp.concatenate([far_left, rel_bias[:, ::-1], far_right, wrapped], axis=1).astype(F32)


def _attn_a(qkv, rel_bias):
    b, s, _ = qkv.shape
    nq = s // A_Q_TILE
    back = A_LEFT // A_Q_TILE
    blk = (1, A_Q_TILE, A_WIDTH)

    def kv_spec(col, off):
        return pl.BlockSpec(blk, lambda bi, i: (bi, jnp.maximum(i - back + off, 0), col))

    return pl.pallas_call(
        _attn_a_body,
        grid=(b, nq),
        in_specs=[pl.BlockSpec(blk, lambda bi, i: (bi, i, 0))]
        + [kv_spec(1, off) for off in range(back + 1)]
        + [kv_spec(2, off) for off in range(back + 1)]
        + [pl.BlockSpec((A_HEADS, A_TOEP), lambda bi, i: (0, 0))],
        out_specs=pl.BlockSpec(blk, lambda bi, i: (bi, i, 0)),
        out_shape=jax.ShapeDtypeStruct((b, s, A_WIDTH), BF16),
        scratch_shapes=[pltpu.VMEM((A_HEADS, A_Q_TILE, A_WIN), F32)],
        compiler_params=_cparams(("arbitrary", "arbitrary")),
        name="attn_a",
    )(qkv, qkv, qkv, qkv, qkv, qkv, qkv, _attn_a_toeplitz_row(rel_bias))


def _attn_b_body(qn_ref, qp_ref, kn_ref, kpe_ref, vt_ref, o_ref, m_ref, acc_ref, s_ref):
    qi = pl.program_id(2)
    t = qn_ref.shape[1]
    slabs = [slice(g * LANE, (g + 1) * LANE) for g in range(B_GROUP)]
    qs = [jnp.concatenate([qn_ref[0, :, sl], qp_ref[0, :, sl]], axis=-1) for sl in slabs]
    shift = int(math.log2(CHUNK))
    key_chunk = lax.broadcasted_iota(jnp.int32, (t, t), 0) >> shift
    qry_chunk = lax.broadcasted_iota(jnp.int32, (t, t), 1) >> shift
    diag_allowed = key_chunk <= qry_chunk

    m_ref[...] = jnp.full_like(m_ref, NEG_INF)
    acc_ref[...] = jnp.zeros_like(acc_ref)

    def scores_into(j, slot):
        rows = pl.ds(pl.multiple_of(j * t, t), t)
        kpe = kpe_ref[0, rows, :]
        for g, sl in enumerate(slabs):
            s_ref[slot, g] = _dot_nt(jnp.concatenate([kn_ref[0, rows, sl], kpe], axis=-1), qs[g])

    def consume(j, slot, masked):
        for g in range(B_GROUP):
            st = s_ref[slot, g]
            if masked:
                st = jnp.where(diag_allowed, st, NEG_INF)
            m_old = m_ref[g]
            m_new = jnp.maximum(m_old, jnp.max(st, axis=0, keepdims=True))
            corr = jnp.exp2(m_old - m_new)
            p = jnp.exp2(st - m_new).astype(BF16)
            m_ref[g] = m_new
            acc_ref[g] = corr * acc_ref[g] + _dot(vt_ref[0, g, j], p)

    scores_into(0, 0)

    def pair_step(p, carry):
        j = 2 * p
        scores_into(j + 1, 1)
        consume(j, 0, False)
        scores_into(j + 2, 0)
        consume(j + 1, 1, False)
        return carry

    lax.fori_loop(0, qi // 2, pair_step, 0)

    @pl.when(qi % 2 == 0)
    def _():
        consume(qi, 0, True)

    @pl.when(qi % 2 == 1)
    def _():
        scores_into(qi, 1)
        consume(qi - 1, 0, False)
        consume(qi, 1, True)
    for g, sl in enumerate(slabs):
        o_ref[0, :, sl] = (acc_ref[g, :V_HEAD, :] / acc_ref[g, V_HEAD:V_HEAD + 1, :]).T.astype(o_ref.dtype)


def _attn_b(q_nope, q_pe, k_nope, k_pe, v_t):
    b, s, _ = q_nope.shape
    t = v_t.shape[-1]
    width = B_GROUP * LANE
    qblk = (1, t, width)
    return pl.pallas_call(
        _attn_b_body,
        grid=(b, B_HEADS // B_GROUP, s // t),
        in_specs=[
            pl.BlockSpec(qblk, lambda bi, hg, i: (bi, i, hg)),
            pl.BlockSpec(qblk, lambda bi, hg, i: (bi, i, hg)),
            pl.BlockSpec((1, s, width), lambda bi, hg, i: (bi, 0, hg)),
            pl.BlockSpec((1, s, LANE), lambda bi, hg, i: (bi, 0, 0)),
            pl.BlockSpec((1, B_GROUP, s // t, B_VT_ROWS, t), lambda bi, hg, i: (bi, hg, 0, 0, 0)),
        ],
        out_specs=pl.BlockSpec(qblk, lambda bi, hg, i: (bi, i, hg)),
        out_shape=jax.ShapeDtypeStruct((b, s, B_WIDTH), BF16),
        scratch_shapes=[pltpu.VMEM((B_GROUP, 1, t), F32), pltpu.VMEM((B_GROUP, B_VT_ROWS, t), F32),
                        pltpu.VMEM((2, B_GROUP, t, t), F32)],
        compiler_params=_cparams(("parallel", "parallel", "arbitrary")),
        name="attn_b",
    )(q_nope, q_pe, k_nope, k_pe, v_t)


def _gate_mix_body(oa_ref, ob_ref, ga_ref, gb_ref, wa_ref, wb_ref, z_ref):
    ya = _dot(oa_ref[...], wa_ref[...])
    yb = _dot(ob_ref[...], wb_ref[...])
    z_ref[...] = (ga_ref[...].astype(F32) * ya + gb_ref[...].astype(F32) * yb).astype(z_ref.dtype)


def _gate_mix(o_a, o_b, gates, w_o_a, w_o_b):
    m = o_a.shape[0]
    tm = min(MM_ROW_TILE, m)
    tn = 1024
    nj = D_MODEL // tn
    return pl.pallas_call(
        _gate_mix_body,
        grid=(m // tm, nj),
        in_specs=[
            pl.BlockSpec((tm, A_WIDTH), lambda i, j: (i, 0)),
            pl.BlockSpec((tm, B_WIDTH), lambda i, j: (i, 0)),
            pl.BlockSpec((tm, tn), lambda i, j: (i, j)),
            pl.BlockSpec((tm, tn), lambda i, j: (i, nj + j)),
            pl.BlockSpec((A_WIDTH, tn), lambda i, j: (0, j)),
            pl.BlockSpec((B_WIDTH, tn), lambda i, j: (0, j)),
        ],
        out_specs=pl.BlockSpec((tm, tn), lambda i, j: (i, j)),
        out_shape=jax.ShapeDtypeStruct((m, D_MODEL), BF16),
        compiler_params=_cparams(("parallel", "parallel")),
        name="gate_mix",
    )(o_a, o_b, gates, gates, w_o_a.astype(BF16), w_o_b.astype(BF16))


def _proj_res_ln_body(x_ref, w_ref, res_ref, g_ref, b_ref, o_ref):
    y = ALPHA * res_ref[...] + _dot(x_ref[...], w_ref[...])
    o_ref[...] = _layer_norm(y, g_ref[...], b_ref[...])


def _proj_res_ln(x, w, res, ln_g, ln_b):
    m, k = x.shape
    tm = min(ROW_TILE, m)
    row = lambda i: (i, 0)
    const = lambda i: (0, 0)
    return pl.pallas_call(
        _proj_res_ln_body,
        grid=(m // tm,),
        in_specs=[
            pl.BlockSpec((tm, k), row),
            pl.BlockSpec((k, D_MODEL), const),
            pl.BlockSpec((tm, D_MODEL), row),
            pl.BlockSpec((1, D_MODEL), const),
            pl.BlockSpec((1, D_MODEL), const),
        ],
        out_specs=pl.BlockSpec((tm, D_MODEL), row),
        out_shape=jax.ShapeDtypeStruct((m, D_MODEL), F32),
        compiler_params=_cparams(("parallel",)),
        name="proj_res_ln",
    )(x, w.astype(BF16), res, ln_g.reshape(1, -1), ln_b.reshape(1, -1))


def _xattn_body(q_ref, kv_ref, w_ref, res_ref, g_ref, b_ref, o_ref, att_ref):
    for h in range(X_HEADS):
        sl = slice(h * X_HEAD_DIM, (h + 1) * X_HEAD_DIM)
        vs = slice(D_MODEL + h * X_HEAD_DIM, D_MODEL + (h + 1) * X_HEAD_DIM)
        s = _dot_nt(q_ref[0, :, sl], kv_ref[0, :, sl])
        mx = jnp.max(s, axis=-1, keepdims=True)
        e = jnp.exp(s - mx)
        den = jnp.sum(e, axis=-1, keepdims=True)
        att_ref[:, sl] = (_dot(e.astype(BF16), kv_ref[0, :, vs]) / den).astype(BF16)
    y = ALPHA * res_ref[0] + _dot(att_ref[...], w_ref[...])
    o_ref[0] = _layer_norm(y, g_ref[...], b_ref[...])


def _xattn(q, kv, w_xo, res, ln_g, ln_b):
    b, s, _ = q.shape
    tm = min(ROW_TILE, s)
    m_len = kv.shape[1]
    row = lambda bi, i: (bi, i, 0)
    return pl.pallas_call(
        _xattn_body,
        grid=(b, s // tm),
        in_specs=[
            pl.BlockSpec((1, tm, D_MODEL), row),
            pl.BlockSpec((1, m_len, 2 * D_MODEL), lambda bi, i: (bi, 0, 0)),
            pl.BlockSpec((D_MODEL, D_MODEL), lambda bi, i: (0, 0)),
            pl.BlockSpec((1, tm, D_MODEL), row),
            pl.BlockSpec((1, D_MODEL), lambda bi, i: (0, 0)),
            pl.BlockSpec((1, D_MODEL), lambda bi, i: (0, 0)),
        ],
        out_specs=pl.BlockSpec((1, tm, D_MODEL), row),
        out_shape=jax.ShapeDtypeStruct((b, s, D_MODEL), F32),
        scratch_shapes=[pltpu.VMEM((tm, D_MODEL), BF16)],
        compiler_params=_cparams(("parallel", "parallel")),
        name="xattn",
    )(q, kv, w_xo.astype(BF16), res, ln_g.reshape(1, -1), ln_b.reshape(1, -1))


def _in_proj_weights(w_in):
    qkv_end = 3 * A_WIDTH
    lat_end = qkv_end + Q_LORA + KV_LORA
    half = QK_ROPE // 2
    w_qkv = w_in[:, :qkv_end].astype(BF16)
    x1 = w_in[:, lat_end:lat_end + half]
    x2 = w_in[:, lat_end + half:lat_end + QK_ROPE]
    kpe_a, kpe_b = _rot_pair(x1, x2, (D_MODEL, LANE - QK_ROPE))
    w_lat = jnp.concatenate([w_in[:, qkv_end:lat_end], kpe_a, kpe_b], axis=1).astype(BF16)
    w_gate = w_in[:, lat_end + QK_ROPE:].astype(BF16)
    return w_qkv, w_lat, w_gate


def kernel(x, mem, ffn1_w_in, ffn1_w_out, ln_ffn1_g, ln_ffn1_b, w_in, gate_bias, rel_bias, q_a_norm, w_q_b,
           kv_a_norm, w_kv_b, w_o_a, w_o_b, w_out, ln_mix_g, ln_mix_b, mem_ln_g, mem_ln_b, w_xq, w_xkv, w_xo,
           ln_x_g, ln_x_b, ffn2_w_in, ffn2_w_out, ln_ffn2_g, ln_ffn2_b):
    b, s, d = x.shape
    m = b * s
    h = x.reshape(m, d)
    for l in range(DEPTH):
        h = _ffn_ln(h, ffn1_w_in[l], ffn1_w_out[l], ln_ffn1_g[l], ln_ffn1_b[l])

        w_qkv, w_lat, w_gate = _in_proj_weights(w_in[l])
        qkv_scale = jnp.concatenate([jnp.full((A_WIDTH,), A_HEAD_DIM ** -0.5 * LOG2_E, F32),
                                     jnp.ones((2 * A_WIDTH,), F32)])
        qkv = _matmul(h, w_qkv, out_dtype=BF16, tn=1024, col_scale=qkv_scale, name="in_proj_qkv")
        gates = _matmul(h, w_gate, out_dtype=BF16, tn=1024, bias=gate_bias[l], act="sigmoid", name="in_proj_gates")
        lat = _matmul(h, w_lat, out_dtype=F32, tn=LAT_COLS, name="in_proj_latent")
        o_a = _attn_a(qkv.reshape(b, s, 3 * A_WIDTH), rel_bias[l])
        q_nope, q_pe, k_nope, v_t, k_pe = _latent_up(lat, q_a_norm[l], w_q_b[l], kv_a_norm[l], w_kv_b[l], s)
        o_b = _attn_b(q_nope.reshape(b, s, B_WIDTH), q_pe.reshape(b, s, B_WIDTH),
                      k_nope.reshape(b, s, B_WIDTH), k_pe.reshape(b, s, LANE), v_t)
        z = _gate_mix(o_a.reshape(m, A_WIDTH), o_b.reshape(m, B_WIDTH), gates, w_o_a[l], w_o_b[l])
        h = _proj_res_ln(z, w_out[l], h, ln_mix_g[l], ln_mix_b[l])

        m_len = mem.shape[1]
        kv_x = _matmul(mem.reshape(b * m_len, d), w_xkv[l].astype(BF16), out_dtype=BF16, tn=1024,
                       ln_g=mem_ln_g[l], ln_b=mem_ln_b[l], name="mem_kv")
        q_x = _matmul(h, w_xq[l].astype(BF16), out_dtype=BF16, tn=1024, out_scale=X_HEAD_DIM ** -0.5, name="xattn_q")
        h = _xattn(q_x.reshape(b, s, d), kv_x.reshape(b, m_len, 2 * d), w_xo[l], h.reshape(b, s, d),
                   ln_x_g[l], ln_x_b[l]).reshape(m, d)

        h = _ffn_ln(h, ffn2_w_in[l], ffn2_w_out[l], ln_ffn2_g[l], ln_ffn2_b[l])
    return h.reshape(b, s, d)
```
